```python
import math
import jax, jax.numpy as jnp
from jax import lax
import numpy as np

D_MODEL = 1024
BATCH = 4
SEQ = 4096
DEPTH = 2
DEC_BATCH = 128
DEC_SEQ = 1
PAST_LEN = 2048
PAGE_SIZE = 128

A_WIDTH = D_MODEL // 2
A_GROUPS = 4
A_GROUP_DIM = A_WIDTH // A_GROUPS
A_CHUNK = 128
B_HEADS = 4
B_KDIM = 128
B_WIDTH = D_MODEL // 2
B_VDIM = B_WIDTH // B_HEADS
B_FDIM = B_HEADS * B_KDIM
B_CHUNK = 64
C_HEADS = 8
C_HEAD_DIM = 64
C_WIDTH = C_HEADS * C_HEAD_DIM
C_QBLOCK = 128
C_BIAS_INIT = -6.0
N_BRANCH = 3
D_FF = 4 * D_MODEL
IN_COLS = 2 * A_WIDTH + 2 * B_FDIM + 2 * B_WIDTH + 3 * C_WIDTH + N_BRANCH * D_MODEL
EPS = 1e-6

kernel_name = "hybrid_gmlp_hgrn2_stickbreaking_decoder_step"


def _in_split_points():
    widths = (A_WIDTH, A_WIDTH, B_FDIM, B_FDIM, B_WIDTH, B_WIDTH, C_WIDTH, C_WIDTH, C_WIDTH)
    pts, acc = [], 0
    for w in widths:
        acc += w
        pts.append(acc)
    return pts


def rmsnorm(x, w):
    xf = x.astype(jnp.float32)
    y = xf * lax.rsqrt(jnp.mean(xf * xf, axis=-1, keepdims=True) + EPS)
    return (y * w.astype(jnp.float32)).astype(x.dtype)


def chunk_spatial_gating(u, v, w_s, b_s):
    Bn, L = v.shape[:2]
    n = -(-L // A_CHUNK)
    vp = jnp.pad(v, ((0, 0), (0, n * A_CHUNK - L), (0, 0), (0, 0)))
    vp = vp.reshape(Bn, n, A_CHUNK, A_GROUPS, A_GROUP_DIM)
    causal = jnp.tril(jnp.ones((A_CHUNK, A_CHUNK), bool))
    w = jnp.where(causal[None], w_s, 0)
    mixed = jnp.einsum('gts,bnsgc->bntgc', w, vp) + b_s.T[None, None, :, :, None]
    mixed = mixed.reshape(Bn, n * A_CHUNK, A_GROUPS, A_GROUP_DIM)[:, :L]
    return u * mixed


def hgrn2_recurrence(q, k, v, log_f, S0):
    Bn, L = q.shape[:2]
    C = min(B_CHUNK, L)
    n = -(-L // C)
    pad = ((0, 0), (0, n * C - L), (0, 0), (0, 0))

    def chunks(t):
        t = jnp.pad(t, pad)
        return t.reshape(Bn, n, C, *t.shape[2:]).swapaxes(0, 1)

    qc, kc, vc, gc = chunks(q), chunks(k), chunks(v), chunks(log_f)
    causal = jnp.tril(jnp.ones((C, C), bool))[None, :, :, None, None]

    def step(S, xs):
        qi, ki, vi, gi = xs
        b = jnp.cumsum(gi, axis=1)
        b_last = b[:, -1]
        o_inter = jnp.einsum('bthk,bhkv->bthv', qi * jnp.exp(b).astype(qi.dtype), S)
        rel = jnp.where(causal, b[:, :, None] - b[:, None, :], -jnp.inf)
        scores = jnp.einsum('btshk,bshk->btsh', jnp.exp(rel).astype(qi.dtype) * qi[:, :, None], ki)
        o_intra = jnp.einsum('btsh,bshv->bthv', scores, vi)
        k_dec = ki * jnp.exp(b_last[:, None] - b).astype(ki.dtype)
        S_new = jnp.exp(b_last)[..., None].astype(S.dtype) * S + jnp.einsum('bshk,bshv->bhkv', k_dec, vi)
        return S_new, o_inter + o_intra

    S_fin, o = lax.scan(step, S0, (qc, kc, vc, gc))
    o = o.swapaxes(0, 1).reshape(Bn, n * C, *o.shape[3:])[:, :L]
    return o, S_fin


def stick_breaking_attention(q, k, v, bias):
    Bn, Lq = q.shape[:2]
    Lk = k.shape[1]
    qb = min(C_QBLOCK, Lq)
    n = -(-Lq // qb)
    qp = jnp.pad(q, ((0, 0), (0, n * qb - Lq), (0, 0), (0, 0)))
    qp = qp.reshape(Bn, n, qb, C_HEADS, C_HEAD_DIM).swapaxes(0, 1)
    kf = k.astype(jnp.float32)
    kpos = jnp.arange(Lk)
    scale = C_HEAD_DIM ** -0.5
    bias_f = bias.astype(jnp.float32)[None, :, None, None]

    def block(args):
        qblk, i = args
        qpos = (Lk - Lq) + i * qb + jnp.arange(qb)
        z = jnp.einsum('bqhd,bkhd->bhqk', qblk.astype(jnp.float32), kf) * scale + bias_f
        mask = kpos[None, :] < qpos[:, None]
        sp = jnp.where(mask, jax.nn.softplus(z), 0.0)
        later = lax.cumsum(sp, axis=3, reverse=True) - sp
        w = jnp.where(mask, jnp.exp(jax.nn.log_sigmoid(z) - later), 0.0)
        return jnp.einsum('bhqk,bkhd->bqhd', w.astype(v.dtype), v)

    o = lax.map(block, (qp, jnp.arange(n)))
    return o.swapaxes(0, 1).reshape(Bn, n * qb, C_HEADS, C_HEAD_DIM)[:, :Lq]


def decoder_layer(x, c, past_k, past_v, S0, lb, w_ada, b_ada, norm1_w, norm2_w, w_in,
                  vnorm_w, w_s, b_s, sb_bias, onorm_w, w_branch_a, w_branch_b, w_branch_c,
                  w_out, w_ff1, w_ff2):
    Bn, L, _ = x.shape
    ada = jnp.einsum('bd,de->be', jax.nn.silu(c), w_ada) + b_ada
    sh1, sc1, g1, sh2, sc2, g2 = jnp.split(ada[:, None, :], 6, axis=-1)

    h = rmsnorm(x, norm1_w) * (1 + sc1) + sh1
    proj = jnp.einsum('bld,de->ble', h, w_in)
    a_u, a_v, b_q, b_f, b_i, b_g, c_q, c_k, c_v, gate_pre = jnp.split(proj, _in_split_points(), axis=-1)

    u = jax.nn.gelu(a_u).reshape(Bn, L, A_GROUPS, A_GROUP_DIM)
    v_a = rmsnorm(jax.nn.gelu(a_v), vnorm_w)
    o_a = chunk_spatial_gating(u, v_a.reshape(Bn, L, A_GROUPS, A_GROUP_DIM), w_s, b_s)
    o_a = o_a.reshape(Bn, L, A_WIDTH)

    q_b = jax.nn.silu(b_q).reshape(Bn, L, B_HEADS, B_KDIM)
    lbf = lb.astype(jnp.float32)
    log_f = jnp.logaddexp(jnp.log(lbf), jnp.log1p(-lbf) + jax.nn.log_sigmoid(b_f.astype(jnp.float32)))
    log_f = log_f.reshape(Bn, L, B_HEADS, B_KDIM)
    k_b = (1.0 - jnp.exp(log_f)).astype(x.dtype)
    v_b = b_i.reshape(Bn, L, B_HEADS, B_VDIM)
    o_b, S_new = hgrn2_recurrence(q_b, k_b, v_b, log_f, S0)
    o_b = rmsnorm(o_b, onorm_w.reshape(B_HEADS, B_VDIM)) * jax.nn.silu(b_g.reshape(Bn, L, B_HEADS, B_VDIM))
    o_b = o_b.reshape(Bn, L, B_WIDTH)

    q_c = c_q.reshape(Bn, L, C_HEADS, C_HEAD_DIM)
    k_c = c_k.reshape(Bn, L, C_HEADS, C_HEAD_DIM)
    v_c = c_v.reshape(Bn, L, C_HEADS, C_HEAD_DIM)
    if past_k is None:
        k_all, v_all = k_c, v_c
    else:
        k_all = jnp.concatenate([past_k, k_c], axis=1)
        v_all = jnp.concatenate([past_v, v_c], axis=1)
    o_c = stick_breaking_attention(q_c, k_all, v_all, sb_bias).reshape(Bn, L, C_WIDTH)

    g_a, g_b, g_c = jnp.split(jax.nn.sigmoid(gate_pre), N_BRANCH, axis=-1)
    merged = (g_a * jnp.einsum('blw,wd->bld', o_a, w_branch_a)
              + g_b * jnp.einsum('blw,wd->bld', o_b, w_branch_b)
              + g_c * jnp.einsum('blw,wd->bld', o_c, w_branch_c))
    x = x + g1 * jnp.einsum('bld,de->ble', merged, w_out)

    h2 = rmsnorm(x, norm2_w) * (1 + sc2) + sh2
    ff = jnp.square(jax.nn.relu(jnp.einsum('bld,df->blf', h2, w_ff1)))
    x = x + g2 * jnp.einsum('blf,fd->bld', ff, w_ff2)
    return x, k_c, v_c, S_new, v_a


def setup_inputs(seed: int = 0) -> dict:
    key = jax.random.key(seed)
    ks = jax.random.split(key, 32)

    def nrm(k, shape, scale=1.0):
        return jax.random.normal(k, shape, jnp.float32) * scale

    n_pages = PAST_LEN // PAGE_SIZE
    n_used = DEC_BATCH * n_pages
    n_pool = n_used + max(1, n_used // 4)
    page_table = jax.random.permutation(ks[7], n_pool)[:n_used].reshape(DEC_BATCH, n_pages).astype(jnp.int32)

    return {
        "x_prompt": nrm(ks[0], (BATCH, SEQ, D_MODEL)),
        "x_sample": nrm(ks[1], (DEC_BATCH, DEC_SEQ, D_MODEL)),
        "c_prompt": nrm(ks[2], (BATCH, D_MODEL)),
        "c_sample": nrm(ks[3], (DEC_BATCH, D_MODEL)),
        "cache_k": nrm(ks[4], (DEPTH, n_pool, PAGE_SIZE, C_HEADS, C_HEAD_DIM)),
        "cache_v": nrm(ks[5], (DEPTH, n_pool, PAGE_SIZE, C_HEADS, C_HEAD_DIM)),
        "state_hgrn": nrm(ks[6], (DEPTH, DEC_BATCH, B_HEADS, B_KDIM, B_VDIM), 0.5),
        "page_table": page_table,
        "w_ada": nrm(ks[8], (DEPTH, D_MODEL, 6 * D_MODEL), D_MODEL ** -0.5),
        "b_ada": nrm(ks[9], (DEPTH, 6 * D_MODEL), 0.02),
        "norm1_w": 1.0 + nrm(ks[10], (DEPTH, D_MODEL), 0.05),
        "norm2_w": 1.0 + nrm(ks[11], (DEPTH, D_MODEL), 0.05),
        "w_in": nrm(ks[12], (DEPTH, D_MODEL, IN_COLS), D_MODEL ** -0.5),
        "gmlp_vnorm_w": 1.0 + nrm(ks[13], (DEPTH, A_WIDTH), 0.05),
        "gmlp_w_s": nrm(ks[14], (DEPTH, A_GROUPS, A_CHUNK, A_CHUNK), A_CHUNK ** -0.5),
        "gmlp_b_s": nrm(ks[15], (DEPTH, A_GROUPS, A_CHUNK), 0.1),
        "sb_bias": C_BIAS_INIT + nrm(ks[25], (DEPTH, C_HEADS), 0.1),
        "hgrn_lb_logits": nrm(ks[16], (DEPTH, B_FDIM)),
        "hgrn_onorm_w": 1.0 + nrm(ks[17], (DEPTH, B_WIDTH), 0.05),
        "w_branch_a": nrm(ks[18], (DEPTH, A_WIDTH, D_MODEL), A_WIDTH ** -0.5),
        "w_branch_b": nrm(ks[19], (DEPTH, B_WIDTH, D_MODEL), B_WIDTH ** -0.5),
        "w_branch_c": nrm(ks[20], (DEPTH, C_WIDTH, D_MODEL), C_WIDTH ** -0.5),
        "w_out": nrm(ks[21], (DEPTH, D_MODEL, D_MODEL), D_MODEL ** -0.5),
        "w_ff1": nrm(ks[22], (DEPTH, D_MODEL, D_FF), D_MODEL ** -0.5),
        "w_ff2": nrm(ks[23], (DEPTH, D_FF, D_MODEL), D_FF ** -0.5),
        "final_norm_w": 1.0 + nrm(ks[24], (D_MODEL,), 0.05),
    }


def reference(x_prompt, x_sample, c_prompt, c_sample, cache_k, cache_v, state_hgrn, page_table,
              w_ada, b_ada, norm1_w, norm2_w, w_in, gmlp_vnorm_w, gmlp_w_s, gmlp_b_s, sb_bias,
              hgrn_lb_logits, hgrn_onorm_w, w_branch_a, w_branch_b, w_branch_c, w_out,
              w_ff1, w_ff2, final_norm_w):
    lb_cum = jnp.cumsum(jax.nn.softmax(hgrn_lb_logits.astype(jnp.float32), axis=0), axis=0)
    lower_bounds = lb_cum - lb_cum[0:1]

    yp, ys = x_prompt, x_sample
    n_seq_dec = page_table.shape[0]
    S0_prompt = jnp.zeros((x_prompt.shape[0], B_HEADS, B_KDIM, B_VDIM), x_prompt.dtype)
    kp_l, vp_l, Sp_l, ks_l, vs_l, Ss_l, gv_l = [], [], [], [], [], [], []
    for l in range(DEPTH):
        lw = (lower_bounds[l], w_ada[l], b_ada[l], norm1_w[l], norm2_w[l], w_in[l],
              gmlp_vnorm_w[l], gmlp_w_s[l], gmlp_b_s[l], sb_bias[l], hgrn_onorm_w[l],
              w_branch_a[l], w_branch_b[l], w_branch_c[l], w_out[l], w_ff1[l], w_ff2[l])
        yp, kp, vp, Sp, _ = decoder_layer(yp, c_prompt, None, None, S0_prompt, *lw)
        past_k = cache_k[l][page_table].reshape(n_seq_dec, -1, C_HEADS, C_HEAD_DIM)
        past_v = cache_v[l][page_table].reshape(n_seq_dec, -1, C_HEADS, C_HEAD_DIM)
        ys, kS, vS, SS, gv = decoder_layer(ys, c_sample, past_k, past_v, state_hgrn[l], *lw)
        kp_l.append(kp); vp_l.append(vp); Sp_l.append(Sp)
        ks_l.append(kS); vs_l.append(vS); Ss_l.append(SS); gv_l.append(gv)

    y_prompt = rmsnorm(yp, final_norm_w)
    y_sample = rmsnorm(ys, final_norm_w)
    k_prompt = jnp.stack(kp_l)
    v_prompt = jnp.stack(vp_l)
    hgrn_state_prompt = jnp.stack(Sp_l)
    k_sample = jnp.stack(ks_l)
    v_sample = jnp.stack(vs_l)
    hgrn_state_sample = jnp.stack(Ss_l)
    gmlp_v_sample = jnp.stack(gv_l)
    return (y_prompt, y_sample, k_prompt, v_prompt, hgrn_state_prompt, k_sample, v_sample, hgrn_state_sample, gmlp_v_sample)
```

```python
import functools

import jax
import jax.numpy as jnp
from jax import lax
from jax.experimental import pallas as pl
from jax.experimental.pallas import tpu as pltpu

F32 = jnp.float32
BF16 = jnp.bfloat16

D_MODEL = 1024
A_WIDTH = 512
A_GROUPS = 4
A_CHUNK = 128
B_HEADS = 4
B_KDIM = 128
B_VDIM = 128
B_WIDTH = 512
C_HEADS = 8
C_HEAD_DIM = 64
C_WIDTH = 512
D_FF = 4096
PAGE_SIZE = 128
EPS = 1e-6

_SEG = dict(a_u=(0, 512), a_v=(512, 1024), b_q=(1024, 1536), b_f=(1536, 2048), b_i=(2048, 2560),
            b_g=(2560, 3072), c_q=(3072, 3584), c_k=(3584, 4096), c_v=(4096, 4608), gate=(4608, 7680))
IN_COLS = 7680

VMEM_LIMIT = 56 * 1024 * 1024
HGRN_SUB = 16
ATT_TQ = 256


def _rms(x, w):
    return x * lax.rsqrt(jnp.mean(x * x, axis=-1, keepdims=True) + EPS) * w


def _silu(x):
    return x * jax.nn.sigmoid(x)


def _softplus(z):
    return jnp.maximum(z, 0.0) + jnp.log(1.0 + jnp.exp(-jnp.abs(z)))


def _const_spec(shape):
    nd = len(shape)
    return pl.BlockSpec(shape, lambda *_: (0,) * nd, pipeline_mode=pl.Buffered(1))


def _params(sem):
    return pltpu.CompilerParams(dimension_semantics=sem, vmem_limit_bytes=VMEM_LIMIT)


def _ada_kernel(c_ref, w_ref, b_ref, o_ref):
    s = _silu(c_ref[...]).astype(BF16)
    o_ref[0] = jnp.dot(s, w_ref[0].astype(BF16), preferred_element_type=F32) + b_ref[0]


def _ada(c_all, w_ada, b_ada):
    depth, d, n = w_ada.shape
    m = c_all.shape[0]
    tn = 1536
    return pl.pallas_call(
        _ada_kernel,
        grid=(depth, n // tn),
        in_specs=[pl.BlockSpec((m, d), lambda l, j: (0, 0)),
                  pl.BlockSpec((1, d, tn), lambda l, j: (l, 0, j)),
                  pl.BlockSpec((1, 1, tn), lambda l, j: (l, 0, j))],
        out_specs=pl.BlockSpec((1, m, tn), lambda l, j: (l, 0, j)),
        out_shape=jax.ShapeDtypeStruct((depth, m, n), F32),
        compiler_params=_params(("arbitrary", "arbitrary")),
        name="ada",
    )(c_all, w_ada, b_ada.reshape(depth, 1, n))


def _inproj_kernel(x_ref, sh_ref, sc_ref, n1_ref, w_ref, vn_ref, ws_ref, bst_ref, lb_ref, *out_refs, tm, sample):
    if sample:
        oa_ref, qb_ref, lf_ref, vb_ref, gb_ref, qc_ref, k_ref, v_ref, gate_ref, va_ref = out_refs
    else:
        oa_ref, qb_ref, lf_ref, vb_ref, gb_ref, qc_ref, kh_ref, vh_ref, k_ref, v_ref, gate_ref = out_refs
    act = qb_ref.dtype
    x = x_ref[0]
    h = (_rms(x, n1_ref[...]) * (1.0 + sc_ref[0]) + sh_ref[0]).astype(BF16)

    def seg(name):
        a, b = _SEG[name]
        return jnp.dot(h, w_ref[:, a:b], preferred_element_type=F32)

    u = jax.nn.gelu(seg("a_u"))
    va = _rms(jax.nn.gelu(seg("a_v")), vn_ref[...])
    gd = A_WIDTH // A_GROUPS
    if sample:
        va_ref[0] = va
        oa_ref[0] = u * (va * ws_ref[0:1, :] + ws_ref[1:2, :])
    else:
        row = lax.broadcasted_iota(jnp.int32, (A_CHUNK, A_CHUNK), 0)
        col = lax.broadcasted_iota(jnp.int32, (A_CHUNK, A_CHUNK), 1)
        va16 = va.astype(BF16)
        for g in range(A_GROUPS):
            cs = slice(g * gd, (g + 1) * gd)
            wg = jnp.where(col <= row, ws_ref[g], 0.0).astype(BF16)
            for c in range(tm // A_CHUNK):
                rs = slice(c * A_CHUNK, (c + 1) * A_CHUNK)
                mixed = jnp.dot(wg, va16[rs, cs], preferred_element_type=F32) + bst_ref[g]
                oa_ref[0, rs, cs] = (u[rs, cs] * mixed).astype(BF16)

    qb_ref[0] = _silu(seg("b_q")).astype(act)
    bf = seg("b_f")
    lb = lb_ref[...]
    log_sig = jnp.minimum(bf, 0.0) - jnp.log(1.0 + jnp.exp(-jnp.abs(bf)))
    a = jnp.log(lb)
    b = jnp.log(1.0 - lb) + log_sig
    lf_ref[0] = jnp.maximum(a, b) + jnp.log(1.0 + jnp.exp(-jnp.abs(a - b)))
    vb_ref[0] = seg("b_i").astype(act)
    gb_ref[0] = _silu(seg("b_g")).astype(act)

    cq = seg("c_q") * (C_HEAD_DIM ** -0.5)
    ck = seg("c_k")
    cv = seg("c_v")
    k_ref[0] = ck.T
    v_ref[0] = cv.T
    if sample:
        qc_ref[0] = cq
    else:
        for hh in range(C_HEADS):
            hs = slice(hh * C_HEAD_DIM, (hh + 1) * C_HEAD_DIM)
            qc_ref[0, hh] = cq[:, hs].astype(BF16)
            kh_ref[0, hh] = ck[:, hs].astype(BF16)
            vh_ref[0, hh] = cv[:, hs].astype(BF16)

    gate_ref[0] = jax.nn.sigmoid(seg("gate")).astype(BF16)


def _mod_spec(sample, tm, k):
    if sample:
        return pl.BlockSpec((1, tm, D_MODEL), lambda g, i: (g, i, k))
    return pl.BlockSpec((1, 1, D_MODEL), lambda g, i: (g, 0, k))


def _inproj(x, mod, n1, w_in, vn, w_s, b_s, lb, *, sample, tm):
    G, R, D = x.shape
    tok = lambda w: pl.BlockSpec((1, tm, w), lambda g, i: (g, i, 0))
    hm = pl.BlockSpec((1, C_HEADS, tm, C_HEAD_DIM), lambda g, i: (g, 0, i, 0))
    sd = jax.ShapeDtypeStruct
    act = F32 if sample else BF16
    w512 = lambda dt: sd((G, R, 512), dt)
    out_shape = [w512(act), w512(act), w512(F32), w512(act), w512(act)]
    out_specs = [tok(512)] * 5
    if sample:
        out_shape += [w512(F32)]
        out_specs += [tok(512)]
    else:
        out_shape += [sd((G, C_HEADS, R, C_HEAD_DIM), BF16)] * 3
        out_specs += [hm] * 3
    chan = pl.BlockSpec((1, 512, tm), lambda g, i: (g, 0, i))
    out_shape += [sd((G, 512, R), F32), sd((G, 512, R), F32), sd((G, R, 3 * D_MODEL), BF16)]
    out_specs += [chan, chan, tok(3 * D_MODEL)]
    if sample:
        out_shape += [w512(F32)]
        out_specs += [tok(512)]
    return pl.pallas_call(
        functools.partial(_inproj_kernel, tm=tm, sample=sample),
        grid=(G, R // tm),
        in_specs=[tok(D), _mod_spec(sample, tm, 0), _mod_spec(sample, tm, 1),
                  _const_spec((1, D)), _const_spec((D, IN_COLS)), _const_spec((1, A_WIDTH)),
                  _const_spec(w_s.shape), _const_spec(b_s.shape), _const_spec((1, 512))],
        out_specs=out_specs,
        out_shape=out_shape,
        compiler_params=_params(("parallel", "parallel")),
        name="inproj_sample" if sample else "inproj_prompt",
    )(x, mod, mod, n1, w_in, vn, w_s, b_s, lb)


def _hgrn_kernel(q_ref, lf_ref, v_ref, gb_ref, on_ref, ob_ref, s_ref, st_scr, *, T, C):
    i = pl.program_id(1)

    @pl.when(i == 0)
    def _():
        st_scr[...] = jnp.zeros_like(st_scr)

    log2c = C.bit_length() - 1
    r_i = lax.broadcasted_iota(jnp.int32, (C, C), 0)
    c_i = lax.broadcasted_iota(jnp.int32, (C, C), 1)
    tri = jnp.where(c_i <= r_i, 1.0, 0.0).astype(BF16)
    sr = lax.broadcasted_iota(jnp.int32, (C, C * C), 0)
    sc = lax.broadcasted_iota(jnp.int32, (C, C * C), 1)
    sel = jnp.where((lax.shift_right_logical(sc, log2c) == sr) & ((sc & (C - 1)) <= sr), 1.0, 0.0).astype(BF16)
    ones = jnp.ones((B_KDIM, B_KDIM), BF16)

    def body(j, carry):
        r0 = pl.multiple_of(j * C, C)
        g_all = lf_ref[0, pl.ds(r0, C), :]
        g1 = g_all.astype(BF16)
        rem = g_all - g1.astype(F32)
        g2 = rem.astype(BF16)
        g3 = (rem - g2.astype(F32)).astype(BF16)
        b_all = (jnp.dot(tri, g1, preferred_element_type=F32) + jnp.dot(tri, g2, preferred_element_type=F32)
                 + jnp.dot(tri, g3, preferred_element_type=F32))
        q_all = q_ref[0, pl.ds(r0, C), :].astype(F32)
        v_all = v_ref[0, pl.ds(r0, C), :]
        gb_all = gb_ref[0, pl.ds(r0, C), :].astype(F32)
        for h in range(B_HEADS):
            cs = slice(h * B_KDIM, (h + 1) * B_KDIM)
            g = g_all[:, cs]
            b = b_all[:, cs]
            q = q_all[:, cs]
            v16 = v_all[:, cs]
            kk = 1.0 - jnp.exp(g)
            b_last = b[C - 1:C, :]
            st = st_scr[h]
            o_inter = lax.dot_general((q * jnp.exp(b)).astype(BF16), st.astype(BF16),
                                      (((1,), (1,)), ((), ())), preferred_element_type=F32)
            pieces = [q[t:t + 1, :] * kk * jnp.exp(jnp.minimum(b[t:t + 1, :] - b, 0.0)) for t in range(C)]
            pair = jnp.concatenate(pieces, axis=0).astype(BF16)
            score = jnp.dot(pair, ones, preferred_element_type=F32)
            v_tiled = jnp.concatenate([v16.astype(F32)] * C, axis=0)
            o_intra = jnp.dot(sel, (score * v_tiled).astype(BF16), preferred_element_type=F32)
            o = o_inter + o_intra
            o = _rms(o, on_ref[:, cs]) * gb_all[:, cs]
            ob_ref[0, pl.ds(r0, C), cs] = o.astype(BF16)
            k_dec = (kk * jnp.exp(b_last - b)).astype(BF16)
            upd = lax.dot_general(v16, k_dec, (((0,), (0,)), ((), ())), preferred_element_type=F32)
            st_scr[h] = st * jnp.exp(b_last) + upd
        return carry

    lax.fori_loop(0, T // C, body, 0)

    @pl.when(i == pl.num_programs(1) - 1)
    def _():
        for h in range(B_HEADS):
            s_ref[0, h] = st_scr[h].T


def _hgrn_prompt(q, lf, v, gb, onorm, *, T):
    G, R, W = q.shape
    tok = pl.BlockSpec((1, T, W), lambda g, i: (g, i, 0))
    return pl.pallas_call(
        functools.partial(_hgrn_kernel, T=T, C=HGRN_SUB),
        grid=(G, R // T),
        in_specs=[tok, tok, tok, tok, _const_spec((1, W))],
        out_specs=[tok, pl.BlockSpec((1, B_HEADS, B_KDIM, B_VDIM), lambda g, i: (g, 0, 0, 0))],
        out_shape=[jax.ShapeDtypeStruct((G, R, W), BF16),
                   jax.ShapeDtypeStruct((G, B_HEADS, B_KDIM, B_VDIM), F32)],
        scratch_shapes=[pltpu.VMEM((B_HEADS, B_VDIM, B_KDIM), F32)],
        compiler_params=_params(("parallel", "arbitrary")),
        name="hgrn_prompt",
    )(q, lf, v, gb, onorm)


def _hgrn_step_kernel(q_ref, lf_ref, v_ref, gb_ref, on_ref, s0_ref, ob_ref, s1_ref):
    eye = (lax.broadcasted_iota(jnp.int32, (B_KDIM, B_KDIM), 0)
           == lax.broadcasted_iota(jnp.int32, (B_KDIM, B_KDIM), 1))
    f_row = jnp.exp(lf_ref[0])
    q_row = q_ref[0]
    v_row = v_ref[0]
    for h in range(B_HEADS):
        cs = slice(h * B_KDIM, (h + 1) * B_KDIM)
        f_col = jnp.sum(jnp.where(eye, jnp.broadcast_to(f_row[:, cs], (B_KDIM, B_KDIM)), 0.0),
                        axis=1, keepdims=True)
        s_new = f_col * s0_ref[0, 0, h] + (1.0 - f_col) * v_row[:, cs]
        s1_ref[0, 0, h] = s_new
        q8 = jnp.broadcast_to(q_row[:, cs], (8, B_KDIM)).astype(BF16)
        o = jnp.dot(q8, s_new.astype(BF16), preferred_element_type=F32)[0:1, :]
        ob_ref[0, :, cs] = _rms(o, on_ref[:, cs]) * gb_ref[0][:, cs]


def _hgrn_sample(q, lf, v, gb, onorm, state, layer):
    n, W = q.shape
    row = pl.BlockSpec((1, 1, W), lambda b: (b, 0, 0))
    st = pl.BlockSpec((1, 1, B_HEADS, B_KDIM, B_VDIM), lambda b: (layer, b, 0, 0, 0))
    st_out = pl.BlockSpec((1, 1, B_HEADS, B_KDIM, B_VDIM), lambda b: (0, b, 0, 0, 0))
    r3 = lambda a: a.reshape(n, 1, W)
    ob, s1 = pl.pallas_call(
        _hgrn_step_kernel,
        grid=(n,),
        in_specs=[row, row, row, row, pl.BlockSpec((1, W), lambda b: (0, 0)), st],
        out_specs=[row, st_out],
        out_shape=[jax.ShapeDtypeStruct((n, 1, W), F32),
                   jax.ShapeDtypeStruct((1, n, B_HEADS, B_KDIM, B_VDIM), F32)],
        compiler_params=_params(("parallel",)),
        name="hgrn_sample",
    )(r3(q), r3(lf), r3(v), r3(gb), onorm, state)
    return ob.reshape(n, W), s1[0]


def _attn_kernel(bias_ref, q_ref, k_ref, v_ref, o_ref, *, tq):
    hp = pl.program_id(1)
    i = pl.program_id(2)
    r_i = lax.broadcasted_iota(jnp.int32, (tq, tq), 0)
    c_i = lax.broadcasted_iota(jnp.int32, (tq, tq), 1)
    upper = jnp.where(r_i > c_i, 1.0, 0.0).astype(BF16)
    nt = (((1,), (1,)), ((), ()))

    def block(hh, j, later, acc, diag):
        q = q_ref[0, hh]
        r0 = pl.multiple_of(j * tq, tq)
        kb = k_ref[0, hh, pl.ds(r0, tq), :]
        vb = v_ref[0, hh, pl.ds(r0, tq), :]
        z = lax.dot_general(q, kb, nt, preferred_element_type=F32) + bias_ref[2 * hp + hh]
        sp = _softplus(z)
        if diag:
            sp = jnp.where(c_i < r_i, sp, 0.0)
        within = jnp.dot(sp.astype(BF16), upper, preferred_element_type=F32)
        w = jnp.exp(z - sp - within - later)
        if diag:
            w = jnp.where(c_i < r_i, w, 0.0)
        acc = acc + jnp.dot(w.astype(BF16), vb, preferred_element_type=F32)
        later = later + jnp.sum(sp, axis=1, keepdims=True)
        return later, acc

    state = []
    for hh in range(2):
        state += list(block(hh, i, jnp.zeros((tq, 1), F32), jnp.zeros((tq, C_HEAD_DIM), F32), True))

    def body(t, st):
        j = i - 1 - t
        out = []
        for hh in range(2):
            out += list(block(hh, j, st[2 * hh], st[2 * hh + 1], False))
        return tuple(out)

    state = lax.fori_loop(0, i, body, tuple(state))
    o_ref[0] = jnp.concatenate([state[1], state[3]], axis=1).astype(BF16)


def _attn_prompt(bias, q, k, v, *, tq):
    G, H, R, Dh = q.shape
    return pl.pallas_call(
        functools.partial(_attn_kernel, tq=tq),
        grid=(G, H // 2, R // tq),
        in_specs=[pl.BlockSpec(memory_space=pltpu.SMEM),
                  pl.BlockSpec((1, 2, tq, Dh), lambda g, hp, i: (g, hp, i, 0)),
                  pl.BlockSpec((1, 2, R, Dh), lambda g, hp, i: (g, hp, 0, 0)),
                  pl.BlockSpec((1, 2, R, Dh), lambda g, hp, i: (g, hp, 0, 0))],
        out_specs=pl.BlockSpec((1, tq, 2 * Dh), lambda g, hp, i: (g, i, hp)),
        out_shape=jax.ShapeDtypeStruct((G, R, H * Dh), BF16),
        compiler_params=_params(("parallel", "parallel", "arbitrary")),
        name="attn_prompt",
    )(bias, q, k, v)


def _attn_page_kernel(pt_ref, q_ref, bias_ref, k_ref, v_ref, o_ref, acc_scr, later_scr):
    j = pl.program_id(1)

    @pl.when(j == 0)
    def _():
        acc_scr[...] = jnp.zeros_like(acc_scr)
        later_scr[...] = jnp.zeros_like(later_scr)

    head_of_lane = lax.shift_right_logical(lax.broadcasted_iota(jnp.int32, (C_HEADS, C_WIDTH), 1), 6)
    head_of_row = lax.broadcasted_iota(jnp.int32, (C_HEADS, C_WIDTH), 0)
    own = head_of_lane == head_of_row
    q_blk = jnp.where(own, jnp.broadcast_to(q_ref[0], (C_HEADS, C_WIDTH)), 0.0).astype(BF16)
    kp = k_ref[0, 0].astype(BF16)
    vp = v_ref[0, 0].astype(BF16)
    z = jnp.dot(q_blk, kp, preferred_element_type=F32) + bias_ref[...]
    sp = _softplus(z)
    r_i = lax.broadcasted_iota(jnp.int32, (PAGE_SIZE, PAGE_SIZE), 0)
    c_i = lax.broadcasted_iota(jnp.int32, (PAGE_SIZE, PAGE_SIZE), 1)
    upper = jnp.where(r_i > c_i, 1.0, 0.0).astype(BF16)
    sp_hi = sp.astype(BF16)
    sp_lo = (sp - sp_hi.astype(F32)).astype(BF16)
    within = jnp.dot(sp_hi, upper, preferred_element_type=F32) + jnp.dot(sp_lo, upper, preferred_element_type=F32)
    w = jnp.exp(z - sp - within - later_scr[...])
    acc_scr[...] += lax.dot_general(w.astype(BF16), vp, (((1,), (1,)), ((), ())), preferred_element_type=F32)
    later_scr[...] += jnp.sum(sp, axis=1, keepdims=True)

    @pl.when(j == pl.num_programs(1) - 1)
    def _():
        o_ref[0] = jnp.sum(jnp.where(own, acc_scr[...], 0.0), axis=0, keepdims=True)


def _attn_sample(page_table, q, bias, cache_k, cache_v, layer):
    n, n_pages = page_table.shape
    depth, n_pool = cache_k.shape[:2]
    ck = jnp.transpose(cache_k, (0, 1, 3, 4, 2)).reshape(depth, n_pool, C_WIDTH, PAGE_SIZE)
    cv = jnp.transpose(cache_v, (0, 1, 3, 4, 2)).reshape(depth, n_pool, C_WIDTH, PAGE_SIZE)
    page = pl.BlockSpec((1, 1, C_WIDTH, PAGE_SIZE), lambda b, j, pt: (layer, pt[b, n_pages - 1 - j], 0, 0))
    row = pl.BlockSpec((1, 1, C_WIDTH), lambda b, j, pt: (b, 0, 0))
    return pl.pallas_call(
        _attn_page_kernel,
        grid_spec=pltpu.PrefetchScalarGridSpec(
            num_scalar_prefetch=1,
            grid=(n, n_pages),
            in_specs=[row, pl.BlockSpec((C_HEADS, 1), lambda b, j, pt: (0, 0)), page, page],
            out_specs=row,
            scratch_shapes=[pltpu.VMEM((C_HEADS, C_WIDTH), F32), pltpu.VMEM((C_HEADS, 1), F32)]),
        out_shape=jax.ShapeDtypeStruct((n, 1, C_WIDTH), F32),
        compiler_params=_params(("arbitrary", "arbitrary")),
        name="attn_sample",
    )(page_table, q.reshape(n, 1, C_WIDTH), bias.reshape(C_HEADS, 1), ck, cv)


def _ffn_kernel(x_ref, oa_ref, ob_ref, oc_ref, gate_ref, g1_ref, sh2_ref, sc2_ref, g2_ref,
                wa_ref, wb_ref, wc_ref, wo_ref, n2_ref, w1_ref, w2_ref, fn_ref, y_ref, *, final, ff_chunk):
    D = D_MODEL
    dot = functools.partial(jnp.dot, preferred_element_type=F32)
    merged = (gate_ref[0, :, 0:D].astype(F32) * dot(oa_ref[0].astype(BF16), wa_ref[...])
              + gate_ref[0, :, D:2 * D].astype(F32) * dot(ob_ref[0].astype(BF16), wb_ref[...])
              + gate_ref[0, :, 2 * D:3 * D].astype(F32) * dot(oc_ref[0].astype(BF16), wc_ref[...]))
    x1 = x_ref[0] + g1_ref[0] * dot(merged.astype(BF16), wo_ref[...])
    h2 = (_rms(x1, n2_ref[...]) * (1.0 + sc2_ref[0]) + sh2_ref[0]).astype(BF16)
    y = jnp.zeros_like(x1)
    for c in range(D_FF // ff_chunk):
        cs = slice(c * ff_chunk, (c + 1) * ff_chunk)
        a = jnp.maximum(dot(h2, w1_ref[:, cs]), 0.0)
        y = y + dot((a * a).astype(BF16), w2_ref[cs, :])
    x2 = x1 + g2_ref[0] * y
    y_ref[0] = _rms(x2, fn_ref[...]) if final else x2


def _ffn(x, oa, ob, oc, gate, mod, wa, wb, wc, wo, n2, w1, w2, fn, *, sample, tm, final):
    G, R, D = x.shape
    tok = lambda w: pl.BlockSpec((1, tm, w), lambda g, i: (g, i, 0))
    return pl.pallas_call(
        functools.partial(_ffn_kernel, final=final, ff_chunk=1024),
        grid=(G, R // tm),
        in_specs=[tok(D), tok(512), tok(512), tok(512), tok(3 * D),
                  _mod_spec(sample, tm, 2), _mod_spec(sample, tm, 3), _mod_spec(sample, tm, 4),
                  _mod_spec(sample, tm, 5),
                  _const_spec((512, D)), _const_spec((512, D)), _const_spec((512, D)), _const_spec((D, D)),
                  _const_spec((1, D)), _const_spec((D, D_FF)), _const_spec((D_FF, D)), _const_spec((1, D))],
        out_specs=tok(D),
        out_shape=jax.ShapeDtypeStruct((G, R, D), F32),
        compiler_params=_params(("parallel", "parallel")),
        name="ffn_sample" if sample else "ffn_prompt",
    )(x, oa, ob, oc, gate, mod, mod, mod, mod, wa, wb, wc, wo, n2, w1, w2, fn)


def _token_major(x):
    G, _, R = x.shape
    return jnp.transpose(x.reshape(G, C_HEADS, C_HEAD_DIM, R), (0, 3, 1, 2))


def kernel(x_prompt, x_sample, c_prompt, c_sample, cache_k, cache_v, state_hgrn, page_table, w_ada, b_ada,
           norm1_w, norm2_w, w_in, gmlp_vnorm_w, gmlp_w_s, gmlp_b_s, sb_bias, hgrn_lb_logits, hgrn_onorm_w,
           w_branch_a, w_branch_b, w_branch_c, w_out, w_ff1, w_ff2, final_norm_w):
    depth = w_in.shape[0]
    nb, seq, d = x_prompt.shape
    ns = x_sample.shape[0]

    lb_cum = jnp.cumsum(jax.nn.softmax(hgrn_lb_logits.astype(F32), axis=0), axis=0)
    lower_bounds = lb_cum - lb_cum[0:1]

    n_c = nb + ns
    pad = (-n_c) % 8
    c_all = jnp.concatenate([c_prompt, c_sample, jnp.zeros((pad, d), F32)], axis=0)
    ada = _ada(c_all, w_ada, b_ada)

    bf = lambda w: w.astype(BF16)
    w_in16, wa16, wb16, wc16 = bf(w_in), bf(w_branch_a), bf(w_branch_b), bf(w_branch_c)
    wo16, w116, w216 = bf(w_out), bf(w_ff1), bf(w_ff2)
    fn = final_norm_w.reshape(1, d)
    gd = A_WIDTH // A_GROUPS

    yp = x_prompt
    ys = x_sample.reshape(1, ns, d)
    kp_l, vp_l, sp_l, ks_l, vs_l, ss_l, gv_l = [], [], [], [], [], [], []
    for l in range(depth):
        final = l == depth - 1
        mod_p = ada[l, :nb].reshape(nb, 1, 6 * d)
        mod_s = ada[l, nb:n_c].reshape(1, ns, 6 * d)
        n1 = norm1_w[l].reshape(1, d)
        n2 = norm2_w[l].reshape(1, d)
        vn = gmlp_vnorm_w[l].reshape(1, A_WIDTH)
        lb = lower_bounds[l].reshape(1, 512)
        onorm = hgrn_onorm_w[l].reshape(1, B_WIDTH)
        b_tile = jnp.broadcast_to(gmlp_b_s[l][:, :, None], (A_GROUPS, A_CHUNK, gd))
        first = jnp.stack([jnp.repeat(gmlp_w_s[l, :, 0, 0], gd), jnp.repeat(gmlp_b_s[l, :, 0], gd)])

        oa, qb, lf, vb, gb, qc, kh, vh, k_c, v_c, gate = _inproj(
            yp, mod_p, n1, w_in16[l], vn, gmlp_w_s[l], b_tile, lb, sample=False, tm=256)
        ob, s_fin = _hgrn_prompt(qb, lf, vb, gb, onorm, T=256)
        oc = _attn_prompt(sb_bias[l], qc, kh, vh, tq=ATT_TQ)
        yp = _ffn(yp, oa, ob, oc, gate, mod_p, wa16[l], wb16[l], wc16[l], wo16[l], n2, w116[l], w216[l], fn,
                  sample=False, tm=256, final=final)
        kp_l.append(_token_major(k_c))
        vp_l.append(_token_major(v_c))
        sp_l.append(s_fin)

        oa, qb, lf, vb, gb, qc, k_c, v_c, gate, va = _inproj(
            ys, mod_s, n1, w_in16[l], vn, first, b_tile[:1, :8], lb, sample=True, tm=ns)
        ob, s_new = _hgrn_sample(qb[0], lf[0], vb[0], gb[0], onorm, state_hgrn, l)
        oc = _attn_sample(page_table, qc[0], sb_bias[l], cache_k, cache_v, l)
        ys = _ffn(ys, oa, ob[None], oc.reshape(1, ns, C_WIDTH), gate, mod_s, wa16[l], wb16[l], wc16[l], wo16[l],
                  n2, w116[l], w216[l], fn, sample=True, tm=ns, final=final)
        ks_l.append(_token_major(k_c).reshape(ns, 1, C_HEADS, C_HEAD_DIM))
        vs_l.append(_token_major(v_c).reshape(ns, 1, C_HEADS, C_HEAD_DIM))
        ss_l.append(s_new)
        gv_l.append(va.reshape(ns, 1, A_WIDTH))

    return (yp, ys.reshape(ns, 1, d), jnp.stack(kp_l), jnp.stack(vp_l), jnp.stack(sp_l),
            jnp.stack(ks_l), jnp.stack(vs_l), jnp.stack(ss_l), jnp.stack(gv_l))
```

```python
import functools

import jax
import jax.numpy as jnp
from jax import lax
from jax.experimental import pallas as pl
from jax.experimental.pallas import tpu as pltpu

F32 = jnp.float32
BF16 = jnp.bfloat16

D_MODEL = 1024
A_WIDTH = 512
A_GROUPS = 4
A_CHUNK = 128
B_HEADS = 4
B_KDIM = 128
B_VDIM = 128
B_WIDTH = 512
C_HEADS = 8
C_HEAD_DIM = 64
C_WIDTH = 512
D_FF = 4096
PAGE_SIZE = 128
EPS = 1e-6

_SEG = dict(a_u=(0, 512), a_v=(512, 1024), b_q=(1024, 1536), b_f=(1536, 2048), b_i=(2048, 2560),
            b_g=(2560, 3072), c_q=(3072, 3584), c_k=(3584, 4096), c_v=(4096, 4608), gate=(4608, 7680))
IN_COLS = 7680

VMEM_LIMIT = 56 * 1024 * 1024
HGRN_SUB = 16
ATT_TQ = 256
LOG2E = 1.4426950408889634
MASKED_EXPONENT = -1e30


def _rms(x, w):
    return x * lax.rsqrt(jnp.mean(x * x, axis=-1, keepdims=True) + EPS) * w


def _silu(x):
    return x * jax.nn.sigmoid(x)


def _softplus(z):
    return jnp.maximum(z, 0.0) + jnp.log(1.0 + jnp.exp(-jnp.abs(z)))


def _const_spec(shape):
    nd = len(shape)
    return pl.BlockSpec(shape, lambda *_: (0,) * nd, pipeline_mode=pl.Buffered(1))


def _params(sem):
    return pltpu.CompilerParams(dimension_semantics=sem, vmem_limit_bytes=VMEM_LIMIT)


def _ada_kernel(c_ref, w_ref, b_ref, o_ref):
    s = _silu(c_ref[...]).astype(BF16)
    o_ref[0] = jnp.dot(s, w_ref[0].astype(BF16), preferred_element_type=F32) + b_ref[0]


def _ada(c_all, w_ada, b_ada):
    depth, d, n = w_ada.shape
    m = c_all.shape[0]
    tn = 1536
    return pl.pallas_call(
        _ada_kernel,
        grid=(depth, n // tn),
        in_specs=[pl.BlockSpec((m, d), lambda l, j: (0, 0)),
                  pl.BlockSpec((1, d, tn), lambda l, j: (l, 0, j)),
                  pl.BlockSpec((1, 1, tn), lambda l, j: (l, 0, j))],
        out_specs=pl.BlockSpec((1, m, tn), lambda l, j: (l, 0, j)),
        out_shape=jax.ShapeDtypeStruct((depth, m, n), F32),
        compiler_params=_params(("arbitrary", "arbitrary")),
        name="ada",
    )(c_all, w_ada, b_ada.reshape(depth, 1, n))


def _inproj_kernel(x_ref, sh_ref, sc_ref, n1_ref, w_ref, vn_ref, ws_ref, bst_ref, lb_ref, *out_refs, tm, sample):
    if sample:
        oa_ref, qb_ref, lf_ref, vb_ref, gb_ref, qc_ref, k_ref, v_ref, gate_ref, va_ref = out_refs
    else:
        oa_ref, qb_ref, lf_ref, vb_ref, gb_ref, qc_ref, kh_ref, vh_ref, k_ref, v_ref, gate_ref = out_refs
    act = qb_ref.dtype
    x = x_ref[0]
    h = (_rms(x, n1_ref[...]) * (1.0 + sc_ref[0]) + sh_ref[0]).astype(BF16)

    def seg(name):
        a, b = _SEG[name]
        return jnp.dot(h, w_ref[:, a:b], preferred_element_type=F32)

    u = jax.nn.gelu(seg("a_u"))
    va = _rms(jax.nn.gelu(seg("a_v")), vn_ref[...])
    gd = A_WIDTH // A_GROUPS
    if sample:
        va_ref[0] = va
        oa_ref[0] = u * (va * ws_ref[0:1, :] + ws_ref[1:2, :])
    else:
        row = lax.broadcasted_iota(jnp.int32, (A_CHUNK, A_CHUNK), 0)
        col = lax.broadcasted_iota(jnp.int32, (A_CHUNK, A_CHUNK), 1)
        va16 = va.astype(BF16)
        for g in range(A_GROUPS):
            cs = slice(g * gd, (g + 1) * gd)
            wg = jnp.where(col <= row, ws_ref[g], 0.0).astype(BF16)
            for c in range(tm // A_CHUNK):
                rs = slice(c * A_CHUNK, (c + 1) * A_CHUNK)
                mixed = jnp.dot(wg, va16[rs, cs], preferred_element_type=F32) + bst_ref[g]
                oa_ref[0, rs, cs] = (u[rs, cs] * mixed).astype(BF16)

    qb_ref[0] = _silu(seg("b_q")).astype(act)
    bf = seg("b_f")
    lb = lb_ref[...]
    log_sig = jnp.minimum(bf, 0.0) - jnp.log(1.0 + jnp.exp(-jnp.abs(bf)))
    a = jnp.log(lb)
    b = jnp.log(1.0 - lb) + log_sig
    lf_ref[0] = jnp.maximum(a, b) + jnp.log(1.0 + jnp.exp(-jnp.abs(a - b)))
    vb_ref[0] = seg("b_i").astype(act)
    gb_ref[0] = _silu(seg("b_g")).astype(act)

    cq = seg("c_q") * (C_HEAD_DIM ** -0.5 * (1.0 if sample else LOG2E))
    ck = seg("c_k")
    cv = seg("c_v")
    k_ref[0] = ck.T
    v_ref[0] = cv.T
    if sample:
        qc_ref[0] = cq.T
    else:
        for hh in range(C_HEADS):
            hs = slice(hh * C_HEAD_DIM, (hh + 1) * C_HEAD_DIM)
            qc_ref[0, hh] = cq[:, hs].astype(BF16)
            kh_ref[0, hh] = ck[:, hs].astype(BF16)
            vh_ref[0, hh] = cv[:, hs].astype(BF16)

    gate_ref[0] = jax.nn.sigmoid(seg("gate")).astype(BF16)


def _mod_spec(sample, tm, k):
    if sample:
        return pl.BlockSpec((1, tm, D_MODEL), lambda g, i: (g, i, k))
    return pl.BlockSpec((1, 1, D_MODEL), lambda g, i: (g, 0, k))


def _inproj(x, mod, n1, w_in, vn, w_s, b_s, lb, *, sample, tm):
    G, R, D = x.shape
    tok = lambda w: pl.BlockSpec((1, tm, w), lambda g, i: (g, i, 0))
    hm = pl.BlockSpec((1, C_HEADS, tm, C_HEAD_DIM), lambda g, i: (g, 0, i, 0))
    sd = jax.ShapeDtypeStruct
    act = F32 if sample else BF16
    w512 = lambda dt: sd((G, R, 512), dt)
    out_shape = [w512(act), w512(act), w512(F32), w512(act), w512(act)]
    out_specs = [tok(512)] * 5
    chan = pl.BlockSpec((1, 512, tm), lambda g, i: (g, 0, i))
    if sample:
        out_shape += [sd((G, 512, R), F32)]
        out_specs += [chan]
    else:
        out_shape += [sd((G, C_HEADS, R, C_HEAD_DIM), BF16)] * 3
        out_specs += [hm] * 3
    out_shape += [sd((G, 512, R), F32), sd((G, 512, R), F32), sd((G, R, 3 * D_MODEL), BF16)]
    out_specs += [chan, chan, tok(3 * D_MODEL)]
    if sample:
        out_shape += [w512(F32)]
        out_specs += [tok(512)]
    return pl.pallas_call(
        functools.partial(_inproj_kernel, tm=tm, sample=sample),
        grid=(G, R // tm),
        in_specs=[tok(D), _mod_spec(sample, tm, 0), _mod_spec(sample, tm, 1),
                  _const_spec((1, D)), _const_spec((D, IN_COLS)), _const_spec((1, A_WIDTH)),
                  _const_spec(w_s.shape), _const_spec(b_s.shape), _const_spec((1, 512))],
        out_specs=out_specs,
        out_shape=out_shape,
        compiler_params=_params(("parallel", "parallel")),
        name="inproj_sample" if sample else "inproj_prompt",
    )(x, mod, mod, n1, w_in, vn, w_s, b_s, lb)


def _hgrn_kernel(q_ref, lf_ref, v_ref, gb_ref, on_ref, ob_ref, s_ref, st_scr, a_scr, *, T, C):
    i = pl.program_id(1)

    @pl.when(i == 0)
    def _():
        st_scr[...] = jnp.zeros_like(st_scr)

    dot = functools.partial(jnp.dot, preferred_element_type=F32)
    nt = (((1,), (1,)), ((), ()))
    log2c = C.bit_length() - 1
    lanes = B_KDIM
    r_i = lax.broadcasted_iota(jnp.int32, (T, T), 0)
    c_i = lax.broadcasted_iota(jnp.int32, (T, T), 1)
    same_sub = lax.shift_right_logical(r_i, log2c) == lax.shift_right_logical(c_i, log2c)
    tri_block = jnp.where(c_i <= r_i, 1.0, 0.0).astype(BF16)
    tri_sub = jnp.where((c_i <= r_i) & same_sub, 1.0, 0.0).astype(BF16)

    g = lf_ref[0]
    g1 = g.astype(BF16)
    rem = g - g1.astype(F32)
    g2 = rem.astype(BF16)
    g3 = (rem - g2.astype(F32)).astype(BF16)
    b = dot(tri_block, g1) + dot(tri_block, g2) + dot(tri_block, g3)
    bs = dot(tri_sub, g1) + dot(tri_sub, g2) + dot(tri_sub, g3)
    f = jnp.exp(g)
    kk = 1.0 - f
    q = q_ref[0].astype(F32)
    v16 = v_ref[0]
    b_end = b[T - 1:T, :]
    q_block = (q * jnp.exp(b)).astype(BF16)
    q_sub = (q * jnp.exp(bs)).astype(BF16)
    k_dec = (kk * jnp.exp(b_end - b)).astype(BF16)
    decay = jnp.exp(b_end)
    gb = gb_ref[0].astype(F32)

    rl = lax.broadcasted_iota(jnp.int32, (T, lanes), 0) & (lanes - 1)
    cl = lax.broadcasted_iota(jnp.int32, (T, lanes), 1)
    offset = jnp.where(lax.shift_right_logical(rl, log2c) == lax.shift_right_logical(cl, log2c), cl - rl, 1)
    ones = jnp.ones((lanes, lanes), BF16)

    for h in range(B_HEADS):
        cs = slice(h * B_KDIM, (h + 1) * B_KDIM)
        st = st_scr[h]
        o = lax.dot_general(q_block[:, cs], st.astype(BF16), nt, preferred_element_type=F32)

        a_scr[...] = jnp.zeros_like(a_scr)
        for s in range(1, T // C):
            rows = slice(s * C, (s + 1) * C)
            ref_b = b[s * C - 1:s * C, cs]
            k_hat = (kk[0:s * C, cs] * jnp.exp(ref_b - b[0:s * C, cs])).astype(BF16)
            a_scr[rows, 0:s * C] = lax.dot_general(q_sub[rows, cs], k_hat, nt, preferred_element_type=F32)

        qh, fh = q[:, cs], f[:, cs]
        k_shift = kk[:, cs]
        prod = None
        diag = jnp.zeros((T, lanes), F32)
        for delta in range(C):
            if delta == 0:
                pair = qh * k_shift
            else:
                k_shift = pltpu.roll(k_shift, 1, axis=0)
                prod = fh if prod is None else fh * pltpu.roll(prod, 1, axis=0)
                pair = qh * k_shift * prod
            score = dot(pair.astype(BF16), ones)
            diag = jnp.where(offset == -delta, score, diag)
        for d in range(T // lanes):
            sl = slice(d * lanes, (d + 1) * lanes)
            a_scr[sl, sl] += diag[sl, :]

        o = o + dot(a_scr[...].astype(BF16), v16[:, cs])
        ob_ref[0, :, cs] = (_rms(o, on_ref[:, cs]) * gb[:, cs]).astype(BF16)
        upd = lax.dot_general(v16[:, cs], k_dec[:, cs], (((0,), (0,)), ((), ())), preferred_element_type=F32)
        st_scr[h] = st * decay[:, cs] + upd

    @pl.when(i == pl.num_programs(1) - 1)
    def _():
        for h in range(B_HEADS):
            s_ref[0, h] = st_scr[h].T


def _hgrn_prompt(q, lf, v, gb, onorm, *, T):
    G, R, W = q.shape
    tok = pl.BlockSpec((1, T, W), lambda g, i: (g, i, 0))
    return pl.pallas_call(
        functools.partial(_hgrn_kernel, T=T, C=HGRN_SUB),
        grid=(G, R // T),
        in_specs=[tok, tok, tok, tok, _const_spec((1, W))],
        out_specs=[tok, pl.BlockSpec((1, B_HEADS, B_KDIM, B_VDIM), lambda g, i: (g, 0, 0, 0))],
        out_shape=[jax.ShapeDtypeStruct((G, R, W), BF16),
                   jax.ShapeDtypeStruct((G, B_HEADS, B_KDIM, B_VDIM), F32)],
        scratch_shapes=[pltpu.VMEM((B_HEADS, B_VDIM, B_KDIM), F32), pltpu.VMEM((T, T), F32)],
        compiler_params=_params(("parallel", "arbitrary")),
        name="hgrn_prompt",
    )(q, lf, v, gb, onorm)


def _hgrn_step_kernel(q_ref, lf_ref, v_ref, gb_ref, on_ref, s0_ref, ob_ref, s1_ref):
    eye = (lax.broadcasted_iota(jnp.int32, (B_KDIM, B_KDIM), 0)
           == lax.broadcasted_iota(jnp.int32, (B_KDIM, B_KDIM), 1))
    f_row = jnp.exp(lf_ref[0])
    q_row = q_ref[0]
    v_row = v_ref[0]
    for h in range(B_HEADS):
        cs = slice(h * B_KDIM, (h + 1) * B_KDIM)
        f_col = jnp.sum(jnp.where(eye, jnp.broadcast_to(f_row[:, cs], (B_KDIM, B_KDIM)), 0.0),
                        axis=1, keepdims=True)
        s_new = f_col * s0_ref[0, 0, h] + (1.0 - f_col) * v_row[:, cs]
        s1_ref[0, 0, h] = s_new
        q8 = jnp.broadcast_to(q_row[:, cs], (8, B_KDIM)).astype(BF16)
        o = jnp.dot(q8, s_new.astype(BF16), preferred_element_type=F32)[0:1, :]
        ob_ref[0, :, cs] = _rms(o, on_ref[:, cs]) * gb_ref[0][:, cs]


def _hgrn_sample(q, lf, v, gb, onorm, state, layer):
    n, W = q.shape
    row = pl.BlockSpec((1, 1, W), lambda b: (b, 0, 0))
    st = pl.BlockSpec((1, 1, B_HEADS, B_KDIM, B_VDIM), lambda b: (layer, b, 0, 0, 0))
    st_out = pl.BlockSpec((1, 1, B_HEADS, B_KDIM, B_VDIM), lambda b: (0, b, 0, 0, 0))
    r3 = lambda a: a.reshape(n, 1, W)
    ob, s1 = pl.pallas_call(
        _hgrn_step_kernel,
        grid=(n,),
        in_specs=[row, row, row, row, pl.BlockSpec((1, W), lambda b: (0, 0)), st],
        out_specs=[row, st_out],
        out_shape=[jax.ShapeDtypeStruct((n, 1, W), F32),
                   jax.ShapeDtypeStruct((1, n, B_HEADS, B_KDIM, B_VDIM), F32)],
        compiler_params=_params(("parallel",)),
        name="hgrn_sample",
    )(r3(q), r3(lf), r3(v), r3(gb), onorm, state)
    return ob.reshape(n, W), s1[0]


def _attn_kernel(bias_ref, q_ref, k_ref, v_ref, o_ref, x_scr, *, tq):
    hp = pl.program_id(1)
    i = pl.program_id(2)
    r_i = lax.broadcasted_iota(jnp.int32, (tq, tq), 0)
    c_i = lax.broadcasted_iota(jnp.int32, (tq, tq), 1)
    upper = jnp.where(r_i > c_i, 1.0, 0.0).astype(BF16)
    keep = c_i < r_i
    nt = (((1,), (1,)), ((), ()))

    def scores(hh, j, later, diag):
        r0 = pl.multiple_of(j * tq, tq)
        kb = k_ref[0, hh, pl.ds(r0, tq), :]
        z = lax.dot_general(q_ref[0, hh], kb, nt, preferred_element_type=F32) + bias_ref[2 * hp + hh] * LOG2E
        sp = jnp.maximum(z, 0.0) + jnp.log2(1.0 + jnp.exp2(-jnp.abs(z)))
        if diag:
            sp = jnp.where(keep, sp, 0.0)
        within = jnp.dot(sp.astype(BF16), upper, preferred_element_type=F32)
        x = z - sp - within - later
        x_scr[hh] = jnp.where(keep, x, MASKED_EXPONENT) if diag else x
        return later + jnp.sum(sp, axis=1, keepdims=True)

    def weigh(hh, j, acc):
        r0 = pl.multiple_of(j * tq, tq)
        vb = v_ref[0, hh, pl.ds(r0, tq), :]
        return acc + jnp.dot(jnp.exp2(x_scr[hh]).astype(BF16), vb, preferred_element_type=F32)

    heads = range(2)
    later = [scores(hh, i, jnp.zeros((tq, 1), F32), True) for hh in heads]
    acc = [jnp.zeros((tq, C_HEAD_DIM), F32) for hh in heads]

    def body(t, carry):
        later, acc = carry
        j = i - 1 - t
        acc = [weigh(hh, j + 1, acc[hh]) for hh in heads]
        later = [scores(hh, j, later[hh], False) for hh in heads]
        return later, acc

    later, acc = lax.fori_loop(0, i, body, (later, acc))
    acc = [weigh(hh, 0, acc[hh]) for hh in heads]
    o_ref[0] = jnp.concatenate(acc, axis=1).astype(BF16)


def _attn_prompt(bias, q, k, v, *, tq):
    G, H, R, Dh = q.shape
    return pl.pallas_call(
        functools.partial(_attn_kernel, tq=tq),
        grid=(G, H // 2, R // tq),
        in_specs=[pl.BlockSpec(memory_space=pltpu.SMEM),
                  pl.BlockSpec((1, 2, tq, Dh), lambda g, hp, i: (g, hp, i, 0)),
                  pl.BlockSpec((1, 2, R, Dh), lambda g, hp, i: (g, hp, 0, 0)),
                  pl.BlockSpec((1, 2, R, Dh), lambda g, hp, i: (g, hp, 0, 0))],
        out_specs=pl.BlockSpec((1, tq, 2 * Dh), lambda g, hp, i: (g, i, hp)),
        out_shape=jax.ShapeDtypeStruct((G, R, H * Dh), BF16),
        scratch_shapes=[pltpu.VMEM((2, tq, tq), F32)],
        compiler_params=_params(("parallel", "parallel", "arbitrary")),
        name="attn_prompt",
    )(bias, q, k, v)


def _attn_seq_kernel(pt_ref, qt_ref, bias_ref, *refs, n_pages):
    k_refs = refs[:n_pages]
    v_refs = refs[n_pages:2 * n_pages]
    o_ref, qb_scr, z_scr, w_scr = refs[2 * n_pages:]
    b = pl.program_id(0)
    ns = qt_ref.shape[1]
    sub = 8
    groups = C_HEAD_DIM // sub

    @pl.when(b == 0)
    def _():
        o_ref[...] = jnp.zeros_like(o_ref)

    mine = lax.broadcasted_iota(jnp.int32, (sub, ns), 1) == b
    for r in range(C_WIDTH // sub):
        rows = slice(r * sub, (r + 1) * sub)
        col = jnp.sum(jnp.where(mine, qt_ref[rows, :], 0.0), axis=1, keepdims=True)
        qb_scr[rows, :] = jnp.broadcast_to(col, (sub, PAGE_SIZE))

    for p in range(n_pages):
        for h in range(C_HEADS):
            part = None
            for r in range(groups):
                rows = slice(h * C_HEAD_DIM + r * sub, h * C_HEAD_DIM + (r + 1) * sub)
                t = k_refs[p][0, 0, rows, :] * qb_scr[rows, :]
                part = t if part is None else part + t
            z_scr[h:h + 1, p * PAGE_SIZE:(p + 1) * PAGE_SIZE] = jnp.sum(part, axis=0, keepdims=True)

    z = z_scr[...] + bias_ref[...]
    sp = _softplus(z)
    log_beta = z - sp
    r_i = lax.broadcasted_iota(jnp.int32, (PAGE_SIZE, PAGE_SIZE), 0)
    c_i = lax.broadcasted_iota(jnp.int32, (PAGE_SIZE, PAGE_SIZE), 1)
    upper = jnp.where(r_i > c_i, 1.0, 0.0).astype(BF16)
    later = jnp.zeros((C_HEADS, 1), F32)
    for p in reversed(range(n_pages)):
        ps = slice(p * PAGE_SIZE, (p + 1) * PAGE_SIZE)
        sp_hi = sp[:, ps].astype(BF16)
        sp_lo = (sp[:, ps] - sp_hi.astype(F32)).astype(BF16)
        within = (jnp.dot(sp_hi, upper, preferred_element_type=F32)
                  + jnp.dot(sp_lo, upper, preferred_element_type=F32))
        w_scr[:, ps] = jnp.exp(log_beta[:, ps] - within - later)
        later = later + jnp.sum(sp[:, ps], axis=1, keepdims=True)

    for r in range(C_WIDTH // sub):
        rows = slice(r * sub, (r + 1) * sub)
        h = r // groups
        acc = None
        for p in range(n_pages):
            wb = jnp.broadcast_to(w_scr[h:h + 1, p * PAGE_SIZE:(p + 1) * PAGE_SIZE], (sub, PAGE_SIZE))
            t = v_refs[p][0, 0, rows, :] * wb
            acc = t if acc is None else acc + t
        col = jnp.sum(acc, axis=1, keepdims=True)
        o_ref[rows, :] = jnp.where(mine, col, o_ref[rows, :])


def _attn_sample(page_table, q_t, bias, cache_k, cache_v, layer):
    n, n_pages = page_table.shape
    depth, n_pool = cache_k.shape[:2]
    ck = jnp.transpose(cache_k, (0, 1, 3, 4, 2)).reshape(depth, n_pool, C_WIDTH, PAGE_SIZE)
    cv = jnp.transpose(cache_v, (0, 1, 3, 4, 2)).reshape(depth, n_pool, C_WIDTH, PAGE_SIZE)

    def page(j):
        return pl.BlockSpec((1, 1, C_WIDTH, PAGE_SIZE), lambda b, pt: (layer, pt[b, j], 0, 0))

    whole = lambda shape: pl.BlockSpec(shape, lambda b, pt: (0,) * len(shape))
    pages = [page(j) for j in range(n_pages)]
    return pl.pallas_call(
        functools.partial(_attn_seq_kernel, n_pages=n_pages),
        grid_spec=pltpu.PrefetchScalarGridSpec(
            num_scalar_prefetch=1,
            grid=(n,),
            in_specs=[whole((C_WIDTH, n)), whole((C_HEADS, 1))] + pages + pages,
            out_specs=whole((C_WIDTH, n)),
            scratch_shapes=[pltpu.VMEM((C_WIDTH, PAGE_SIZE), F32),
                            pltpu.VMEM((C_HEADS, n_pages * PAGE_SIZE), F32),
                            pltpu.VMEM((C_HEADS, n_pages * PAGE_SIZE), F32)]),
        out_shape=jax.ShapeDtypeStruct((C_WIDTH, n), F32),
        compiler_params=_params(("arbitrary",)),
        name="attn_sample",
    )(page_table, q_t, bias.reshape(C_HEADS, 1), *([ck] * n_pages), *([cv] * n_pages))


def _ffn_kernel(x_ref, oa_ref, ob_ref, oc_ref, gate_ref, g1_ref, sh2_ref, sc2_ref, g2_ref,
                wa_ref, wb_ref, wc_ref, wo_ref, n2_ref, w1_ref, w2_ref, fn_ref, y_ref, *, final, ff_chunk,
                oc_channel_major):
    D = D_MODEL
    dot = functools.partial(jnp.dot, preferred_element_type=F32)
    oc = oc_ref[0].T if oc_channel_major else oc_ref[0]
    merged = (gate_ref[0, :, 0:D].astype(F32) * dot(oa_ref[0].astype(BF16), wa_ref[...])
              + gate_ref[0, :, D:2 * D].astype(F32) * dot(ob_ref[0].astype(BF16), wb_ref[...])
              + gate_ref[0, :, 2 * D:3 * D].astype(F32) * dot(oc.astype(BF16), wc_ref[...]))
    x1 = x_ref[0] + g1_ref[0] * dot(merged.astype(BF16), wo_ref[...])
    h2 = (_rms(x1, n2_ref[...]) * (1.0 + sc2_ref[0]) + sh2_ref[0]).astype(BF16)
    y = jnp.zeros_like(x1)
    for c in range(D_FF // ff_chunk):
        cs = slice(c * ff_chunk, (c + 1) * ff_chunk)
        a = jnp.maximum(dot(h2, w1_ref[:, cs]), 0.0)
        y = y + dot((a * a).astype(BF16), w2_ref[cs, :])
    x2 = x1 + g2_ref[0] * y
    y_ref[0] = _rms(x2, fn_ref[...]) if final else x2


def _ffn(x, oa, ob, oc, gate, mod, wa, wb, wc, wo, n2, w1, w2, fn, *, sample, tm, final):
    G, R, D = x.shape
    tok = lambda w: pl.BlockSpec((1, tm, w), lambda g, i: (g, i, 0))
    return pl.pallas_call(
        functools.partial(_ffn_kernel, final=final, ff_chunk=1024, oc_channel_major=sample),
        grid=(G, R // tm),
        in_specs=[tok(D), tok(512), tok(512),
                  pl.BlockSpec((1, 512, tm), lambda g, i: (g, 0, i)) if sample else tok(512), tok(3 * D),
                  _mod_spec(sample, tm, 2), _mod_spec(sample, tm, 3), _mod_spec(sample, tm, 4),
                  _mod_spec(sample, tm, 5),
                  _const_spec((512, D)), _const_spec((512, D)), _const_spec((512, D)), _const_spec((D, D)),
                  _const_spec((1, D)), _const_spec((D, D_FF)), _const_spec((D_FF, D)), _const_spec((1, D))],
        out_specs=tok(D),
        out_shape=jax.ShapeDtypeStruct((G, R, D), F32),
        compiler_params=_params(("parallel", "parallel")),
        name="ffn_sample" if sample else "ffn_prompt",
    )(x, oa, ob, oc, gate, mod, mod, mod, mod, wa, wb, wc, wo, n2, w1, w2, fn)


def _token_major(x):
    G, _, R = x.shape
    return jnp.transpose(x.reshape(G, C_HEADS, C_HEAD_DIM, R), (0, 3, 1, 2))


def kernel(x_prompt, x_sample, c_prompt, c_sample, cache_k, cache_v, state_hgrn, page_table, w_ada, b_ada,
           norm1_w, norm2_w, w_in, gmlp_vnorm_w, gmlp_w_s, gmlp_b_s, sb_bias, hgrn_lb_logits, hgrn_onorm_w,
           w_branch_a, w_branch_b, w_branch_c, w_out, w_ff1, w_ff2, final_norm_w):
    depth = w_in.shape[0]
    nb, seq, d = x_prompt.shape
    ns = x_sample.shape[0]

    lb_cum = jnp.cumsum(jax.nn.softmax(hgrn_lb_logits.astype(F32), axis=0), axis=0)
    lower_bounds = lb_cum - lb_cum[0:1]

    n_c = nb + ns
    pad = (-n_c) % 8
    c_all = jnp.concatenate([c_prompt, c_sample, jnp.zeros((pad, d), F32)], axis=0)
    ada = _ada(c_all, w_ada, b_ada)

    bf = lambda w: w.astype(BF16)
    w_in16, wa16, wb16, wc16 = bf(w_in), bf(w_branch_a), bf(w_branch_b), bf(w_branch_c)
    wo16, w116, w216 = bf(w_out), bf(w_ff1), bf(w_ff2)
    fn = final_norm_w.reshape(1, d)
    gd = A_WIDTH // A_GROUPS

    yp = x_prompt
    ys = x_sample.reshape(1, ns, d)
    kp_l, vp_l, sp_l, ks_l, vs_l, ss_l, gv_l = [], [], [], [], [], [], []
    for l in range(depth):
        final = l == depth - 1
        mod_p = ada[l, :nb].reshape(nb, 1, 6 * d)
        mod_s = ada[l, nb:n_c].reshape(1, ns, 6 * d)
        n1 = norm1_w[l].reshape(1, d)
        n2 = norm2_w[l].reshape(1, d)
        vn = gmlp_vnorm_w[l].reshape(1, A_WIDTH)
        lb = lower_bounds[l].reshape(1, 512)
        onorm = hgrn_onorm_w[l].reshape(1, B_WIDTH)
        b_tile = jnp.broadcast_to(gmlp_b_s[l][:, :, None], (A_GROUPS, A_CHUNK, gd))
        first = jnp.stack([jnp.repeat(gmlp_w_s[l, :, 0, 0], gd), jnp.repeat(gmlp_b_s[l, :, 0], gd)])

        oa, qb, lf, vb, gb, qc, kh, vh, k_c, v_c, gate = _inproj(
            yp, mod_p, n1, w_in16[l], vn, gmlp_w_s[l], b_tile, lb, sample=False, tm=256)
        ob, s_fin = _hgrn_prompt(qb, lf, vb, gb, onorm, T=256)
        oc = _attn_prompt(sb_bias[l], qc, kh, vh, tq=ATT_TQ)
        yp = _ffn(yp, oa, ob, oc, gate, mod_p, wa16[l], wb16[l], wc16[l], wo16[l], n2, w116[l], w216[l], fn,
                  sample=False, tm=256, final=final)
        kp_l.append(_token_major(k_c))
        vp_l.append(_token_major(v_c))
        sp_l.append(s_fin)

        oa, qb, lf, vb, gb, qc, k_c, v_c, gate, va = _inproj(
            ys, mod_s, n1, w_in16[l], vn, first, b_tile[:1, :8], lb, sample=True, tm=ns)
        ob, s_new = _hgrn_sample(qb[0], lf[0], vb[0], gb[0], onorm, state_hgrn, l)
        oc = _attn_sample(page_table, qc[0], sb_bias[l], cache_k, cache_v, l)
        ys = _ffn(ys, oa, ob[None], oc[None], gate, mod_s, wa16[l], wb16[l], wc16[l], wo16[l],
                  n2, w116[l], w216[l], fn, sample=True, tm=ns, final=final)
        ks_l.append(_token_major(k_c).reshape(ns, 1, C_HEADS, C_HEAD_DIM))
        vs_l.append(_token_major(v_c).reshape(ns, 1, C_HEADS, C_HEAD_DIM))
        ss_l.append(s_new)
        gv_l.append(va.reshape(ns, 1, A_WIDTH))

    return (yp, ys.reshape(ns, 1, d), jnp.stack(kp_l), jnp.stack(vp_l), jnp.stack(sp_l),
            jnp.stack(ks_l), jnp.stack(vs_l), jnp.stack(ss_l), jnp.stack(gv_l))
```

```python
import functools

import jax
import jax.numpy as jnp
from jax import lax
from jax.experimental import pallas as pl
from jax.experimental.pallas import tpu as pltpu

F32 = jnp.float32
BF16 = jnp.bfloat16

D_MODEL = 1024
A_WIDTH = 512
A_GROUPS = 4
A_CHUNK = 128
B_HEADS = 4
B_KDIM = 128
B_VDIM = 128
B_WIDTH = 512
C_HEADS = 8
C_HEAD_DIM = 64
C_WIDTH = 512
D_FF = 4096
PAGE_SIZE = 128
EPS = 1e-6

_SEG = dict(a_u=(0, 512), a_v=(512, 1024), b_q=(1024, 1536), b_f=(1536, 2048), b_i=(2048, 2560),
            b_g=(2560, 3072), c_q=(3072, 3584), c_k=(3584, 4096), c_v=(4096, 4608), gate=(4608, 7680))
IN_COLS = 7680

VMEM_LIMIT = 56 * 1024 * 1024
HGRN_SUB = 16
ATT_TQ = 256
ATT_HEADS = 4
LOG2E = 1.4426950408889634
MASKED_EXPONENT = -1e30


def _rms(x, w):
    return x * lax.rsqrt(jnp.mean(x * x, axis=-1, keepdims=True) + EPS) * w


def _silu(x):
    return x * jax.nn.sigmoid(x)


def _softplus(z):
    return jnp.maximum(z, 0.0) + jnp.log(1.0 + jnp.exp(-jnp.abs(z)))


def _const_spec(shape):
    nd = len(shape)
    return pl.BlockSpec(shape, lambda *_: (0,) * nd, pipeline_mode=pl.Buffered(1))


def _params(sem):
    return pltpu.CompilerParams(dimension_semantics=sem, vmem_limit_bytes=VMEM_LIMIT)


def _ada_kernel(c_ref, w_ref, b_ref, o_ref):
    s = _silu(c_ref[...]).astype(BF16)
    o_ref[0] = jnp.dot(s, w_ref[0].astype(BF16), preferred_element_type=F32) + b_ref[0]


def _ada(c_all, w_ada, b_ada):
    depth, d, n = w_ada.shape
    m = c_all.shape[0]
    tn = 1536
    return pl.pallas_call(
        _ada_kernel,
        grid=(depth, n // tn),
        in_specs=[pl.BlockSpec((m, d), lambda l, j: (0, 0)),
                  pl.BlockSpec((1, d, tn), lambda l, j: (l, 0, j)),
                  pl.BlockSpec((1, 1, tn), lambda l, j: (l, 0, j))],
        out_specs=pl.BlockSpec((1, m, tn), lambda l, j: (l, 0, j)),
        out_shape=jax.ShapeDtypeStruct((depth, m, n), F32),
        compiler_params=_params(("arbitrary", "arbitrary")),
        name="ada",
    )(c_all, w_ada, b_ada.reshape(depth, 1, n))


def _inproj_kernel(x_ref, sh_ref, sc_ref, n1_ref, w_ref, vn_ref, ws_ref, bst_ref, lb_ref, *out_refs, tm, sample):
    if sample:
        oa_ref, qb_ref, lf_ref, vb_ref, gb_ref, qc_ref, k_ref, v_ref, gate_ref, va_ref = out_refs
    else:
        oa_ref, qb_ref, lf_ref, vb_ref, gb_ref, qc_ref, kh_ref, vh_ref, k_ref, v_ref, gate_ref = out_refs
    act = qb_ref.dtype
    x = x_ref[0]
    h = (_rms(x, n1_ref[...]) * (1.0 + sc_ref[0]) + sh_ref[0]).astype(BF16)

    def seg(name):
        a, b = _SEG[name]
        return jnp.dot(h, w_ref[:, a:b], preferred_element_type=F32)

    u = jax.nn.gelu(seg("a_u"))
    va = _rms(jax.nn.gelu(seg("a_v")), vn_ref[...])
    gd = A_WIDTH // A_GROUPS
    if sample:
        va_ref[0] = va
        oa_ref[0] = u * (va * ws_ref[0:1, :] + ws_ref[1:2, :])
    else:
        row = lax.broadcasted_iota(jnp.int32, (A_CHUNK, A_CHUNK), 0)
        col = lax.broadcasted_iota(jnp.int32, (A_CHUNK, A_CHUNK), 1)
        va16 = va.astype(BF16)
        for g in range(A_GROUPS):
            cs = slice(g * gd, (g + 1) * gd)
            wg = jnp.where(col <= row, ws_ref[g], 0.0).astype(BF16)
            for c in range(tm // A_CHUNK):
                rs = slice(c * A_CHUNK, (c + 1) * A_CHUNK)
                mixed = jnp.dot(wg, va16[rs, cs], preferred_element_type=F32) + bst_ref[g]
                oa_ref[0, rs, cs] = (u[rs, cs] * mixed).astype(BF16)

    qb_ref[0] = _silu(seg("b_q")).astype(act)
    bf = seg("b_f")
    lb = lb_ref[...]
    log_sig = jnp.minimum(bf, 0.0) - jnp.log(1.0 + jnp.exp(-jnp.abs(bf)))
    a = jnp.log(lb)
    b = jnp.log(1.0 - lb) + log_sig
    lf_ref[0] = jnp.maximum(a, b) + jnp.log(1.0 + jnp.exp(-jnp.abs(a - b)))
    vb_ref[0] = seg("b_i").astype(act)
    gb_ref[0] = _silu(seg("b_g")).astype(act)

    cq = seg("c_q") * (C_HEAD_DIM ** -0.5 * (1.0 if sample else LOG2E))
    ck = seg("c_k")
    cv = seg("c_v")
    k_ref[0] = ck.T
    v_ref[0] = cv.T
    if sample:
        qc_ref[0] = cq.T
    else:
        for hh in range(C_HEADS):
            hs = slice(hh * C_HEAD_DIM, (hh + 1) * C_HEAD_DIM)
            qc_ref[0, hh] = cq[:, hs].astype(BF16)
            kh_ref[0, hh] = ck[:, hs].astype(BF16)
            vh_ref[0, hh] = cv[:, hs].astype(BF16)

    gate_ref[0] = jax.nn.sigmoid(seg("gate")).astype(BF16)


def _mod_spec(sample, tm, k):
    if sample:
        return pl.BlockSpec((1, tm, D_MODEL), lambda g, i: (g, i, k))
    return pl.BlockSpec((1, 1, D_MODEL), lambda g, i: (g, 0, k))


def _inproj(x, mod, n1, w_in, vn, w_s, b_s, lb, *, sample, tm):
    G, R, D = x.shape
    tok = lambda w: pl.BlockSpec((1, tm, w), lambda g, i: (g, i, 0))
    hm = pl.BlockSpec((1, C_HEADS, tm, C_HEAD_DIM), lambda g, i: (g, 0, i, 0))
    sd = jax.ShapeDtypeStruct
    act = F32 if sample else BF16
    w512 = lambda dt: sd((G, R, 512), dt)
    out_shape = [w512(act), w512(act), w512(F32), w512(act), w512(act)]
    out_specs = [tok(512)] * 5
    chan = pl.BlockSpec((1, 512, tm), lambda g, i: (g, 0, i))
    if sample:
        out_shape += [sd((G, 512, R), F32)]
        out_specs += [chan]
    else:
        out_shape += [sd((G, C_HEADS, R, C_HEAD_DIM), BF16)] * 3
        out_specs += [hm] * 3
    out_shape += [sd((G, 512, R), F32), sd((G, 512, R), F32), sd((G, R, 3 * D_MODEL), BF16)]
    out_specs += [chan, chan, tok(3 * D_MODEL)]
    if sample:
        out_shape += [w512(F32)]
        out_specs += [tok(512)]
    return pl.pallas_call(
        functools.partial(_inproj_kernel, tm=tm, sample=sample),
        grid=(G, R // tm),
        in_specs=[tok(D), _mod_spec(sample, tm, 0), _mod_spec(sample, tm, 1),
                  _const_spec((1, D)), _const_spec((D, IN_COLS)), _const_spec((1, A_WIDTH)),
                  _const_spec(w_s.shape), _const_spec(b_s.shape), _const_spec((1, 512))],
        out_specs=out_specs,
        out_shape=out_shape,
        compiler_params=_params(("parallel", "parallel")),
        name="inproj_sample" if sample else "inproj_prompt",
    )(x, mod, mod, n1, w_in, vn, w_s, b_s, lb)


def _hgrn_kernel(q_ref, lf_ref, v_ref, gb_ref, on_ref, ob_ref, s_ref, st_scr, a_scr, *, T, C):
    i = pl.program_id(1)

    @pl.when(i == 0)
    def _():
        st_scr[...] = jnp.zeros_like(st_scr)

    dot = functools.partial(jnp.dot, preferred_element_type=F32)
    nt = (((1,), (1,)), ((), ()))
    log2c = C.bit_length() - 1
    lanes = B_KDIM
    r_i = lax.broadcasted_iota(jnp.int32, (T, T), 0)
    c_i = lax.broadcasted_iota(jnp.int32, (T, T), 1)
    same_sub = lax.shift_right_logical(r_i, log2c) == lax.shift_right_logical(c_i, log2c)
    tri_block = jnp.where(c_i <= r_i, 1.0, 0.0).astype(BF16)
    tri_sub = jnp.where((c_i <= r_i) & same_sub, 1.0, 0.0).astype(BF16)

    g = lf_ref[0]
    g1 = g.astype(BF16)
    rem = g - g1.astype(F32)
    g2 = rem.astype(BF16)
    g3 = (rem - g2.astype(F32)).astype(BF16)
    b = dot(tri_block, g1) + dot(tri_block, g2) + dot(tri_block, g3)
    bs = dot(tri_sub, g1) + dot(tri_sub, g2) + dot(tri_sub, g3)
    f = jnp.exp(g)
    kk = 1.0 - f
    q = q_ref[0].astype(F32)
    v16 = v_ref[0]
    b_end = b[T - 1:T, :]
    q_block = (q * jnp.exp(b)).astype(BF16)
    q_sub = (q * jnp.exp(bs)).astype(BF16)
    k_dec = (kk * jnp.exp(b_end - b)).astype(BF16)
    decay = jnp.exp(b_end)
    gb = gb_ref[0].astype(F32)

    rl = lax.broadcasted_iota(jnp.int32, (T, lanes), 0) & (lanes - 1)
    cl = lax.broadcasted_iota(jnp.int32, (T, lanes), 1)
    offset = jnp.where(lax.shift_right_logical(rl, log2c) == lax.shift_right_logical(cl, log2c), cl - rl, 1)
    ones = jnp.ones((lanes, lanes), BF16)

    for h in range(B_HEADS):
        cs = slice(h * B_KDIM, (h + 1) * B_KDIM)
        st = st_scr[h]
        o = lax.dot_general(q_block[:, cs], st.astype(BF16), nt, preferred_element_type=F32)

        a_scr[...] = jnp.zeros_like(a_scr)
        for s in range(1, T // C):
            rows = slice(s * C, (s + 1) * C)
            ref_b = b[s * C - 1:s * C, cs]
            k_hat = (kk[0:s * C, cs] * jnp.exp(ref_b - b[0:s * C, cs])).astype(BF16)
            a_scr[rows, 0:s * C] = lax.dot_general(q_sub[rows, cs], k_hat, nt, preferred_element_type=F32)

        qh, fh = q[:, cs], f[:, cs]
        k_decayed = kk[:, cs]
        diag = jnp.zeros((T, lanes), F32)
        for delta in range(C):
            if delta > 0:
                k_decayed = fh * pltpu.roll(k_decayed, 1, axis=0)
            score = dot((qh * k_decayed).astype(BF16), ones)
            diag = jnp.where(offset == -delta, score, diag)
        for d in range(T // lanes):
            sl = slice(d * lanes, (d + 1) * lanes)
            a_scr[sl, sl] += diag[sl, :]

        o = o + dot(a_scr[...].astype(BF16), v16[:, cs])
        ob_ref[0, :, cs] = (_rms(o, on_ref[:, cs]) * gb[:, cs]).astype(BF16)
        upd = lax.dot_general(v16[:, cs], k_dec[:, cs], (((0,), (0,)), ((), ())), preferred_element_type=F32)
        st_scr[h] = st * decay[:, cs] + upd

    @pl.when(i == pl.num_programs(1) - 1)
    def _():
        for h in range(B_HEADS):
            s_ref[0, h] = st_scr[h].T


def _hgrn_prompt(q, lf, v, gb, onorm, *, T):
    G, R, W = q.shape
    tok = pl.BlockSpec((1, T, W), lambda g, i: (g, i, 0))
    return pl.pallas_call(
        functools.partial(_hgrn_kernel, T=T, C=HGRN_SUB),
        grid=(G, R // T),
        in_specs=[tok, tok, tok, tok, _const_spec((1, W))],
        out_specs=[tok, pl.BlockSpec((1, B_HEADS, B_KDIM, B_VDIM), lambda g, i: (g, 0, 0, 0))],
        out_shape=[jax.ShapeDtypeStruct((G, R, W), BF16),
                   jax.ShapeDtypeStruct((G, B_HEADS, B_KDIM, B_VDIM), F32)],
        scratch_shapes=[pltpu.VMEM((B_HEADS, B_VDIM, B_KDIM), F32), pltpu.VMEM((T, T), F32)],
        compiler_params=_params(("parallel", "arbitrary")),
        name="hgrn_prompt",
    )(q, lf, v, gb, onorm)


def _hgrn_step_kernel(q_ref, lf_ref, v_ref, gb_ref, on_ref, s0_ref, ob_ref, s1_ref, *, n_seq):
    eye = (lax.broadcasted_iota(jnp.int32, (B_KDIM, B_KDIM), 0)
           == lax.broadcasted_iota(jnp.int32, (B_KDIM, B_KDIM), 1))
    f_all = jnp.exp(lf_ref[...])
    for s in range(n_seq):
        row = slice(s, s + 1)
        for h in range(B_HEADS):
            cs = slice(h * B_KDIM, (h + 1) * B_KDIM)
            f_col = jnp.sum(jnp.where(eye, jnp.broadcast_to(f_all[row, cs], (B_KDIM, B_KDIM)), 0.0),
                            axis=1, keepdims=True)
            s_new = f_col * s0_ref[0, s, h] + (1.0 - f_col) * v_ref[row, cs]
            s1_ref[0, s, h] = s_new
            q8 = jnp.broadcast_to(q_ref[row, cs], (8, B_KDIM)).astype(BF16)
            o = jnp.dot(q8, s_new.astype(BF16), preferred_element_type=F32)[0:1, :]
            ob_ref[row, cs] = _rms(o, on_ref[:, cs]) * gb_ref[row, cs]


def _hgrn_sample(q, lf, v, gb, onorm, state, layer, *, n_seq):
    n, W = q.shape
    rows = pl.BlockSpec((n_seq, W), lambda b: (b, 0))
    st = pl.BlockSpec((1, n_seq, B_HEADS, B_KDIM, B_VDIM), lambda b: (layer, b, 0, 0, 0))
    st_out = pl.BlockSpec((1, n_seq, B_HEADS, B_KDIM, B_VDIM), lambda b: (0, b, 0, 0, 0))
    ob, s1 = pl.pallas_call(
        functools.partial(_hgrn_step_kernel, n_seq=n_seq),
        grid=(n // n_seq,),
        in_specs=[rows, rows, rows, rows, pl.BlockSpec((1, W), lambda b: (0, 0)), st],
        out_specs=[rows, st_out],
        out_shape=[jax.ShapeDtypeStruct((n, W), F32),
                   jax.ShapeDtypeStruct((1, n, B_HEADS, B_KDIM, B_VDIM), F32)],
        compiler_params=_params(("parallel",)),
        name="hgrn_sample",
    )(q, lf, v, gb, onorm, state)
    return ob, s1[0]


def _attn_kernel(bias_ref, q_ref, k_ref, v_ref, o_ref, z_scr, x_scr, acc_scr, *, tq, nh):
    hp = pl.program_id(1)
    i = pl.program_id(2)
    r_i = lax.broadcasted_iota(jnp.int32, (tq, tq), 0)
    c_i = lax.broadcasted_iota(jnp.int32, (tq, tq), 1)
    upper = jnp.where(r_i > c_i, 1.0, 0.0).astype(BF16)
    keep = c_i < r_i
    nt = (((1,), (1,)), ((), ()))

    heads = range(nh)

    def logits(j):
        r0 = pl.multiple_of(j * tq, tq)
        for hh in heads:
            kb = k_ref[0, hh, pl.ds(r0, tq), :]
            z = lax.dot_general(q_ref[0, hh], kb, nt, preferred_element_type=F32)
            z_scr[hh] = z + bias_ref[nh * hp + hh] * LOG2E

    def exponents(later, diag):
        out = []
        for hh in heads:
            z = z_scr[hh]
            sp = jnp.maximum(z, 0.0) + jnp.log2(1.0 + jnp.exp2(-jnp.abs(z)))
            if diag:
                sp = jnp.where(keep, sp, 0.0)
            within = jnp.dot(sp.astype(BF16), upper, preferred_element_type=F32)
            x = z - sp - within - later[hh]
            x_scr[hh] = jnp.where(keep, x, MASKED_EXPONENT) if diag else x
            out.append(later[hh] + jnp.sum(sp, axis=1, keepdims=True))
        return out

    def weigh(j):
        r0 = pl.multiple_of(j * tq, tq)
        for hh in heads:
            vb = v_ref[0, hh, pl.ds(r0, tq), :]
            acc_scr[hh] += jnp.dot(jnp.exp2(x_scr[hh]).astype(BF16), vb, preferred_element_type=F32)

    acc_scr[...] = jnp.zeros_like(acc_scr)
    logits(i)
    later = exponents([jnp.zeros((tq, 1), F32) for hh in heads], True)
    logits(jnp.maximum(i - 1, 0))

    def body(m, later):
        weigh(i - m + 2)
        later = exponents(later, False)
        logits(i - m)
        return later

    later = lax.fori_loop(2, i + 1, body, later)

    @pl.when(i >= 1)
    def _():
        weigh(1)
        exponents(later, False)

    weigh(0)
    o_ref[0] = jnp.concatenate([acc_scr[hh] for hh in heads], axis=1).astype(BF16)


def _attn_prompt(bias, q, k, v, *, tq, nh):
    G, H, R, Dh = q.shape
    return pl.pallas_call(
        functools.partial(_attn_kernel, tq=tq, nh=nh),
        grid=(G, H // nh, R // tq),
        in_specs=[pl.BlockSpec(memory_space=pltpu.SMEM),
                  pl.BlockSpec((1, nh, tq, Dh), lambda g, hp, i: (g, hp, i, 0)),
                  pl.BlockSpec((1, nh, R, Dh), lambda g, hp, i: (g, hp, 0, 0)),
                  pl.BlockSpec((1, nh, R, Dh), lambda g, hp, i: (g, hp, 0, 0))],
        out_specs=pl.BlockSpec((1, tq, nh * Dh), lambda g, hp, i: (g, i, hp)),
        out_shape=jax.ShapeDtypeStruct((G, R, H * Dh), BF16),
        scratch_shapes=[pltpu.VMEM((nh, tq, tq), F32), pltpu.VMEM((nh, tq, tq), F32),
                        pltpu.VMEM((nh, tq, Dh), F32)],
        compiler_params=_params(("parallel", "parallel", "arbitrary")),
        name="attn_prompt",
    )(bias, q, k, v)


def _attn_seq_kernel(pt_ref, qt_ref, bias_ref, *refs, n_pages):
    k_refs = refs[:n_pages]
    v_refs = refs[n_pages:2 * n_pages]
    o_ref, qb_scr, z_scr, w_scr = refs[2 * n_pages:]
    b = pl.program_id(0)
    ns = qt_ref.shape[1]
    sub = 8
    groups = C_HEAD_DIM // sub

    @pl.when(b == 0)
    def _():
        o_ref[...] = jnp.zeros_like(o_ref)

    mine = lax.broadcasted_iota(jnp.int32, (sub, ns), 1) == b
    for r in range(C_WIDTH // sub):
        rows = slice(r * sub, (r + 1) * sub)
        col = jnp.sum(jnp.where(mine, qt_ref[rows, :], 0.0), axis=1, keepdims=True)
        qb_scr[rows, :] = jnp.broadcast_to(col, (sub, PAGE_SIZE))

    for p in range(n_pages):
        for h in range(C_HEADS):
            part = None
            for r in range(groups):
                rows = slice(h * C_HEAD_DIM + r * sub, h * C_HEAD_DIM + (r + 1) * sub)
                t = k_refs[p][0, 0, rows, :] * qb_scr[rows, :]
                part = t if part is None else part + t
            z_scr[h:h + 1, p * PAGE_SIZE:(p + 1) * PAGE_SIZE] = jnp.sum(part, axis=0, keepdims=True)

    z = z_scr[...] + bias_ref[...]
    sp = _softplus(z)
    log_beta = z - sp
    r_i = lax.broadcasted_iota(jnp.int32, (PAGE_SIZE, PAGE_SIZE), 0)
    c_i = lax.broadcasted_iota(jnp.int32, (PAGE_SIZE, PAGE_SIZE), 1)
    upper = jnp.where(r_i > c_i, 1.0, 0.0).astype(BF16)
    later = jnp.zeros((C_HEADS, 1), F32)
    for p in reversed(range(n_pages)):
        ps = slice(p * PAGE_SIZE, (p + 1) * PAGE_SIZE)
        sp_hi = sp[:, ps].astype(BF16)
        sp_lo = (sp[:, ps] - sp_hi.astype(F32)).astype(BF16)
        within = (jnp.dot(sp_hi, upper, preferred_element_type=F32)
                  + jnp.dot(sp_lo, upper, preferred_element_type=F32))
        w_scr[:, ps] = jnp.exp(log_beta[:, ps] - within - later)
        later = later + jnp.sum(sp[:, ps], axis=1, keepdims=True)

    for r in range(C_WIDTH // sub):
        rows = slice(r * sub, (r + 1) * sub)
        h = r // groups
        acc = None
        for p in range(n_pages):
            wb = jnp.broadcast_to(w_scr[h:h + 1, p * PAGE_SIZE:(p + 1) * PAGE_SIZE], (sub, PAGE_SIZE))
            t = v_refs[p][0, 0, rows, :] * wb
            acc = t if acc is None else acc + t
        col = jnp.sum(acc, axis=1, keepdims=True)
        o_ref[rows, :] = jnp.where(mine, col, o_ref[rows, :])


def _attn_sample(page_table, q_t, bias, cache_k, cache_v, layer):
    n, n_pages = page_table.shape
    depth, n_pool = cache_k.shape[:2]
    ck = jnp.transpose(cache_k, (0, 1, 3, 4, 2)).reshape(depth, n_pool, C_WIDTH, PAGE_SIZE)
    cv = jnp.transpose(cache_v, (0, 1, 3, 4, 2)).reshape(depth, n_pool, C_WIDTH, PAGE_SIZE)

    def page(j):
        return pl.BlockSpec((1, 1, C_WIDTH, PAGE_SIZE), lambda b, pt: (layer, pt[b, j], 0, 0))

    whole = lambda shape: pl.BlockSpec(shape, lambda b, pt: (0,) * len(shape))
    pages = [page(j) for j in range(n_pages)]
    return pl.pallas_call(
        functools.partial(_attn_seq_kernel, n_pages=n_pages),
        grid_spec=pltpu.PrefetchScalarGridSpec(
            num_scalar_prefetch=1,
            grid=(n,),
            in_specs=[whole((C_WIDTH, n)), whole((C_HEADS, 1))] + pages + pages,
            out_specs=whole((C_WIDTH, n)),
            scratch_shapes=[pltpu.VMEM((C_WIDTH, PAGE_SIZE), F32),
                            pltpu.VMEM((C_HEADS, n_pages * PAGE_SIZE), F32),
                            pltpu.VMEM((C_HEADS, n_pages * PAGE_SIZE), F32)]),
        out_shape=jax.ShapeDtypeStruct((C_WIDTH, n), F32),
        compiler_params=_params(("arbitrary",)),
        name="attn_sample",
    )(page_table, q_t, bias.reshape(C_HEADS, 1), *([ck] * n_pages), *([cv] * n_pages))


def _ffn_kernel(x_ref, oa_ref, ob_ref, oc_ref, gate_ref, g1_ref, sh2_ref, sc2_ref, g2_ref,
                wa_ref, wb_ref, wc_ref, wo_ref, n2_ref, w1_ref, w2_ref, fn_ref, y_ref, *, final, ff_chunk,
                oc_channel_major):
    D = D_MODEL
    dot = functools.partial(jnp.dot, preferred_element_type=F32)
    oc = oc_ref[0].T if oc_channel_major else oc_ref[0]
    merged = (gate_ref[0, :, 0:D].astype(F32) * dot(oa_ref[0].astype(BF16), wa_ref[...])
              + gate_ref[0, :, D:2 * D].astype(F32) * dot(ob_ref[0].astype(BF16), wb_ref[...])
              + gate_ref[0, :, 2 * D:3 * D].astype(F32) * dot(oc.astype(BF16), wc_ref[...]))
    x1 = x_ref[0] + g1_ref[0] * dot(merged.astype(BF16), wo_ref[...])
    h2 = (_rms(x1, n2_ref[...]) * (1.0 + sc2_ref[0]) + sh2_ref[0]).astype(BF16)
    y = jnp.zeros_like(x1)
    for c in range(D_FF // ff_chunk):
        cs = slice(c * ff_chunk, (c + 1) * ff_chunk)
        a = jnp.maximum(dot(h2, w1_ref[:, cs]), 0.0)
        y = y + dot((a * a).astype(BF16), w2_ref[cs, :])
    x2 = x1 + g2_ref[0] * y
    y_ref[0] = _rms(x2, fn_ref[...]) if final else x2


def _ffn(x, oa, ob, oc, gate, mod, wa, wb, wc, wo, n2, w1, w2, fn, *, sample, tm, final):
    G, R, D = x.shape
    tok = lambda w: pl.BlockSpec((1, tm, w), lambda g, i: (g, i, 0))
    return pl.pallas_call(
        functools.partial(_ffn_kernel, final=final, ff_chunk=1024, oc_channel_major=sample),
        grid=(G, R // tm),
        in_specs=[tok(D), tok(512), tok(512),
                  pl.BlockSpec((1, 512, tm), lambda g, i: (g, 0, i)) if sample else tok(512), tok(3 * D),
                  _mod_spec(sample, tm, 2), _mod_spec(sample, tm, 3), _mod_spec(sample, tm, 4),
                  _mod_spec(sample, tm, 5),
                  _const_spec((512, D)), _const_spec((512, D)), _const_spec((512, D)), _const_spec((D, D)),
                  _const_spec((1, D)), _const_spec((D, D_FF)), _const_spec((D_FF, D)), _const_spec((1, D))],
        out_specs=tok(D),
        out_shape=jax.ShapeDtypeStruct((G, R, D), F32),
        compiler_params=_params(("parallel", "parallel")),
        name="ffn_sample" if sample else "ffn_prompt",
    )(x, oa, ob, oc, gate, mod, mod, mod, mod, wa, wb, wc, wo, n2, w1, w2, fn)


def _token_major(x):
    G, _, R = x.shape
    return jnp.transpose(x.reshape(G, C_HEADS, C_HEAD_DIM, R), (0, 3, 1, 2))


def kernel(x_prompt, x_sample, c_prompt, c_sample, cache_k, cache_v, state_hgrn, page_table, w_ada, b_ada,
           norm1_w, norm2_w, w_in, gmlp_vnorm_w, gmlp_w_s, gmlp_b_s, sb_bias, hgrn_lb_logits, hgrn_onorm_w,
           w_branch_a, w_branch_b, w_branch_c, w_out, w_ff1, w_ff2, final_norm_w):
    depth = w_in.shape[0]
    nb, seq, d = x_prompt.shape
    ns = x_sample.shape[0]

    lb_cum = jnp.cumsum(jax.nn.softmax(hgrn_lb_logits.astype(F32), axis=0), axis=0)
    lower_bounds = lb_cum - lb_cum[0:1]

    n_c = nb + ns
    pad = (-n_c) % 8
    c_all = jnp.concatenate([c_prompt, c_sample, jnp.zeros((pad, d), F32)], axis=0)
    ada = _ada(c_all, w_ada, b_ada)

    bf = lambda w: w.astype(BF16)
    w_in16, wa16, wb16, wc16 = bf(w_in), bf(w_branch_a), bf(w_branch_b), bf(w_branch_c)
    wo16, w116, w216 = bf(w_out), bf(w_ff1), bf(w_ff2)
    fn = final_norm_w.reshape(1, d)
    gd = A_WIDTH // A_GROUPS

    yp = x_prompt
    ys = x_sample.reshape(1, ns, d)
    kp_l, vp_l, sp_l, ks_l, vs_l, ss_l, gv_l = [], [], [], [], [], [], []
    for l in range(depth):
        final = l == depth - 1
        mod_p = ada[l, :nb].reshape(nb, 1, 6 * d)
        mod_s = ada[l, nb:n_c].reshape(1, ns, 6 * d)
        n1 = norm1_w[l].reshape(1, d)
        n2 = norm2_w[l].reshape(1, d)
        vn = gmlp_vnorm_w[l].reshape(1, A_WIDTH)
        lb = lower_bounds[l].reshape(1, 512)
        onorm = hgrn_onorm_w[l].reshape(1, B_WIDTH)
        b_tile = jnp.broadcast_to(gmlp_b_s[l][:, :, None], (A_GROUPS, A_CHUNK, gd))
        first = jnp.stack([jnp.repeat(gmlp_w_s[l, :, 0, 0], gd), jnp.repeat(gmlp_b_s[l, :, 0], gd)])

        oa, qb, lf, vb, gb, qc, kh, vh, k_c, v_c, gate = _inproj(
            yp, mod_p, n1, w_in16[l], vn, gmlp_w_s[l], b_tile, lb, sample=False, tm=256)
        ob, s_fin = _hgrn_prompt(qb, lf, vb, gb, onorm, T=256)
        oc = _attn_prompt(sb_bias[l], qc, kh, vh, tq=ATT_TQ, nh=ATT_HEADS)
        yp = _ffn(yp, oa, ob, oc, gate, mod_p, wa16[l], wb16[l], wc16[l], wo16[l], n2, w116[l], w216[l], fn,
                  sample=False, tm=256, final=final)
        kp_l.append(_token_major(k_c))
        vp_l.append(_token_major(v_c))
        sp_l.append(s_fin)

        oa, qb, lf, vb, gb, qc, k_c, v_c, gate, va = _inproj(
            ys, mod_s, n1, w_in16[l], vn, first, b_tile[:1, :8], lb, sample=True, tm=ns)
        ob, s_new = _hgrn_sample(qb[0], lf[0], vb[0], gb[0], onorm, state_hgrn, l, n_seq=8)
        oc = _attn_sample(page_table, qc[0], sb_bias[l], cache_k, cache_v, l)
        ys = _ffn(ys, oa, ob[None], oc[None], gate, mod_s, wa16[l], wb16[l], wc16[l], wo16[l],
                  n2, w116[l], w216[l], fn, sample=True, tm=ns, final=final)
        ks_l.append(_token_major(k_c).reshape(ns, 1, C_HEADS, C_HEAD_DIM))
        vs_l.append(_token_major(v_c).reshape(ns, 1, C_HEADS, C_HEAD_DIM))
        ss_l.append(s_new)
        gv_l.append(va.reshape(ns, 1, A_WIDTH))

    return (yp, ys.reshape(ns, 1, d), jnp.stack(kp_l), jnp.stack(vp_l), jnp.stack(sp_l),
            jnp.stack(ks_l), jnp.stack(vs_l), jnp.stack(ss_l), jnp.stack(gv_l))
```

```python
import functools

import jax
import jax.numpy as jnp
from jax import lax
from jax.experimental import pallas as pl
from jax.experimental.pallas import tpu as pltpu

F32 = jnp.float32
BF16 = jnp.bfloat16

D_MODEL = 1024
A_WIDTH = 512
A_GROUPS = 4
A_CHUNK = 128
B_HEADS = 4
B_KDIM = 128
B_VDIM = 128
B_WIDTH = 512
C_HEADS = 8
C_HEAD_DIM = 64
C_WIDTH = 512
D_FF = 4096
PAGE_SIZE = 128
EPS = 1e-6

_SEG = dict(a_u=(0, 512), a_v=(512, 1024), b_q=(1024, 1536), b_f=(1536, 2048), b_i=(2048, 2560),
            b_g=(2560, 3072), c_q=(3072, 3584), c_k=(3584, 4096), c_v=(4096, 4608), gate=(4608, 7680))
IN_COLS = 7680

VMEM_LIMIT = 56 * 1024 * 1024
HGRN_SUB = 16
HGRN_BLOCK = 256
ATT_TQ = 256
ATT_HEADS = 4
LOG2E = 1.4426950408889634
MASKED_EXPONENT = -1e30


def _rms(x, w):
    return x * lax.rsqrt(jnp.mean(x * x, axis=-1, keepdims=True) + EPS) * w


def _sigmoid(x):
    return 0.5 * jnp.tanh(0.5 * x) + 0.5


def _silu(x):
    return x * _sigmoid(x)


def _softplus(z):
    return jnp.maximum(z, 0.0) + jnp.log(1.0 + jnp.exp(-jnp.abs(z)))


def _const_spec(shape):
    nd = len(shape)
    return pl.BlockSpec(shape, lambda *_: (0,) * nd, pipeline_mode=pl.Buffered(1))


def _params(sem):
    return pltpu.CompilerParams(dimension_semantics=sem, vmem_limit_bytes=VMEM_LIMIT)


def _ada_kernel(c_ref, w_ref, b_ref, o_ref):
    s = _silu(c_ref[...]).astype(BF16)
    o_ref[0] = jnp.dot(s, w_ref[0].astype(BF16), preferred_element_type=F32) + b_ref[0]


def _ada(c_all, w_ada, b_ada):
    depth, d, n = w_ada.shape
    m = c_all.shape[0]
    tn = 1536
    return pl.pallas_call(
        _ada_kernel,
        grid=(depth, n // tn),
        in_specs=[pl.BlockSpec((m, d), lambda l, j: (0, 0)),
                  pl.BlockSpec((1, d, tn), lambda l, j: (l, 0, j)),
                  pl.BlockSpec((1, 1, tn), lambda l, j: (l, 0, j))],
        out_specs=pl.BlockSpec((1, m, tn), lambda l, j: (l, 0, j)),
        out_shape=jax.ShapeDtypeStruct((depth, m, n), F32),
        compiler_params=_params(("arbitrary", "arbitrary")),
        name="ada",
    )(c_all, w_ada, b_ada.reshape(depth, 1, n))


def _inproj_kernel(x_ref, sh_ref, sc_ref, n1_ref, w_ref, vn_ref, ws_ref, bst_ref, lb_ref, *refs, tm, sample):
    if sample:
        oa_ref, qb_ref, lf_ref, vb_ref, gb_ref, qc_ref, k_ref, v_ref, gate_ref, va_ref = refs
    else:
        (oa_ref, qb_ref, kd_ref, vb_ref, gb_ref, oi_ref, dec_ref, qc_ref, kh_ref, vt_ref, k_ref, v_ref, gate_ref,
         a_scr) = refs
    act = qb_ref.dtype
    x = x_ref[0]
    h = (_rms(x, n1_ref[...]) * (1.0 + sc_ref[0]) + sh_ref[0]).astype(BF16)

    def seg(name):
        a, b = _SEG[name]
        return jnp.dot(h, w_ref[:, a:b], preferred_element_type=F32)

    u = jax.nn.gelu(seg("a_u"))
    va = _rms(jax.nn.gelu(seg("a_v")), vn_ref[...])
    gd = A_WIDTH // A_GROUPS
    if sample:
        va_ref[0] = va
        oa_ref[0] = u * (va * ws_ref[0:1, :] + ws_ref[1:2, :])
    else:
        row = lax.broadcasted_iota(jnp.int32, (A_CHUNK, A_CHUNK), 0)
        col = lax.broadcasted_iota(jnp.int32, (A_CHUNK, A_CHUNK), 1)
        va16 = va.astype(BF16)
        for g in range(A_GROUPS):
            cs = slice(g * gd, (g + 1) * gd)
            wg = jnp.where(col <= row, ws_ref[g], 0.0).astype(BF16)
            for c in range(tm // A_CHUNK):
                rs = slice(c * A_CHUNK, (c + 1) * A_CHUNK)
                mixed = jnp.dot(wg, va16[rs, cs], preferred_element_type=F32) + bst_ref[g]
                oa_ref[0, rs, cs] = (u[rs, cs] * mixed).astype(BF16)

    q_b = _silu(seg("b_q"))
    bf = seg("b_f")
    lb = lb_ref[...]
    log_sig = jnp.minimum(bf, 0.0) - jnp.log(1.0 + jnp.exp(-jnp.abs(bf)))
    a = jnp.log(lb)
    b = jnp.log(1.0 - lb) + log_sig
    log_f = jnp.maximum(a, b) + jnp.log(1.0 + jnp.exp(-jnp.abs(a - b)))
    v_b = seg("b_i").astype(act)
    vb_ref[0] = v_b
    gb_ref[0] = _silu(seg("b_g")).astype(act)
    def gate_columns(lo, hi):
        a0 = _SEG["gate"][0]
        z = jnp.dot(h, w_ref[:, a0 + lo:a0 + hi], preferred_element_type=F32)
        gate_ref[0, :, lo:hi] = _sigmoid(z).astype(BF16)

    n_gate = _SEG["gate"][1] - _SEG["gate"][0]
    if sample:
        qb_ref[0] = q_b
        lf_ref[0] = log_f
        gate_columns(0, n_gate)
    else:
        step = n_gate // B_HEADS
        qb_ref[0], kd_ref[0], dec_ref[0, 0], oi_ref[0] = _hgrn_block_local(
            q_b, log_f, v_b, a_scr, C=HGRN_SUB, after_head=lambda hd: gate_columns(hd * step, (hd + 1) * step))

    cq = seg("c_q") * (C_HEAD_DIM ** -0.5 * (1.0 if sample else LOG2E))
    ck = seg("c_k")
    cv = seg("c_v")
    k_ref[0] = ck.T
    cv_t = cv.T
    v_ref[0] = cv_t
    qc_ref[0] = cq.T.astype(qc_ref.dtype)
    if not sample:
        vt_ref[0] = cv_t.astype(BF16)
        for hh in range(C_HEADS):
            kh_ref[0, hh] = ck[:, hh * C_HEAD_DIM:(hh + 1) * C_HEAD_DIM].astype(BF16)


def _mod_spec(sample, tm, k):
    if sample:
        return pl.BlockSpec((1, tm, D_MODEL), lambda g, i: (g, i, k))
    return pl.BlockSpec((1, 1, D_MODEL), lambda g, i: (g, 0, k))


def _inproj(x, mod, n1, w_in, vn, w_s, b_s, lb, *, sample, tm):
    G, R, D = x.shape
    tok = lambda w: pl.BlockSpec((1, tm, w), lambda g, i: (g, i, 0))
    hm = pl.BlockSpec((1, C_HEADS, tm, C_HEAD_DIM), lambda g, i: (g, 0, i, 0))
    sd = jax.ShapeDtypeStruct
    w512 = lambda dt: sd((G, R, 512), dt)
    chan = pl.BlockSpec((1, 512, tm), lambda g, i: (g, 0, i))
    if sample:
        out_shape = [w512(F32)] * 5 + [sd((G, 512, R), F32)]
        out_specs = [tok(512)] * 5 + [chan]
    else:
        out_shape = [w512(BF16)] * 5 + [w512(F32), sd((G, R // tm, 1, 512), F32)]
        out_specs = [tok(512)] * 6 + [pl.BlockSpec((1, 1, 1, 512), lambda g, i: (g, i, 0, 0))]
        out_shape += [sd((G, 512, R), BF16), sd((G, C_HEADS, R, C_HEAD_DIM), BF16), sd((G, 512, R), BF16)]
        out_specs += [chan, hm, chan]
    out_shape += [sd((G, 512, R), F32), sd((G, 512, R), F32), sd((G, R, 3 * D_MODEL), BF16)]
    out_specs += [chan, chan, tok(3 * D_MODEL)]
    if sample:
        out_shape += [w512(F32)]
        out_specs += [tok(512)]
    return pl.pallas_call(
        functools.partial(_inproj_kernel, tm=tm, sample=sample),
        grid=(G, R // tm),
        in_specs=[tok(D), _mod_spec(sample, tm, 0), _mod_spec(sample, tm, 1),
                  _const_spec((1, D)), _const_spec((D, IN_COLS)), _const_spec((1, A_WIDTH)),
                  _const_spec(w_s.shape), _const_spec(b_s.shape), _const_spec((1, 512))],
        out_specs=out_specs,
        out_shape=out_shape,
        scratch_shapes=[] if sample else [pltpu.VMEM((tm, tm), F32)],
        compiler_params=_params(("parallel", "parallel")),
        name="inproj_sample" if sample else "inproj_prompt",
    )(x, mod, mod, n1, w_in, vn, w_s, b_s, lb)


def _hgrn_block_local(q, g, v16, a_scr, *, C, after_head):
    T = q.shape[0]
    dot = functools.partial(jnp.dot, preferred_element_type=F32)
    nt = (((1,), (1,)), ((), ()))
    log2c = C.bit_length() - 1
    lanes = B_KDIM
    r_i = lax.broadcasted_iota(jnp.int32, (T, T), 0)
    c_i = lax.broadcasted_iota(jnp.int32, (T, T), 1)
    same_sub = lax.shift_right_logical(r_i, log2c) == lax.shift_right_logical(c_i, log2c)
    tri_block = jnp.where(c_i <= r_i, 1.0, 0.0).astype(BF16)
    tri_sub = jnp.where((c_i <= r_i) & same_sub, 1.0, 0.0).astype(BF16)

    g1 = g.astype(BF16)
    rem = g - g1.astype(F32)
    g2 = rem.astype(BF16)
    g3 = (rem - g2.astype(F32)).astype(BF16)
    b = dot(tri_block, g1) + dot(tri_block, g2) + dot(tri_block, g3)
    bs = dot(tri_sub, g1) + dot(tri_sub, g2) + dot(tri_sub, g3)
    f = jnp.exp(g)
    kk = 1.0 - f
    b_end = b[T - 1:T, :]
    q_block = (q * jnp.exp(b)).astype(BF16)
    q_sub = (q * jnp.exp(bs)).astype(BF16)
    k_dec = (kk * jnp.exp(b_end - b)).astype(BF16)

    rl = lax.broadcasted_iota(jnp.int32, (T, lanes), 0) & (lanes - 1)
    cl = lax.broadcasted_iota(jnp.int32, (T, lanes), 1)
    offset = jnp.where(lax.shift_right_logical(rl, log2c) == lax.shift_right_logical(cl, log2c), cl - rl, 1)
    ones = jnp.ones((lanes, lanes), BF16)

    outs = []
    for h in range(B_HEADS):
        cs = slice(h * B_KDIM, (h + 1) * B_KDIM)
        a_scr[...] = jnp.zeros_like(a_scr)
        for s in range(1, T // C):
            rows = slice(s * C, (s + 1) * C)
            ref_b = b[s * C - 1:s * C, cs]
            k_hat = (kk[0:s * C, cs] * jnp.exp(ref_b - b[0:s * C, cs])).astype(BF16)
            a_scr[rows, 0:s * C] = lax.dot_general(q_sub[rows, cs], k_hat, nt, preferred_element_type=F32)

        qh, fh = q[:, cs], f[:, cs]
        k_decayed = kk[:, cs]
        diag = jnp.zeros((T, lanes), F32)
        for delta in range(C):
            if delta > 0:
                k_decayed = fh * pltpu.roll(k_decayed, 1, axis=0)
            score = dot((qh * k_decayed).astype(BF16), ones)
            diag = jnp.where(offset == -delta, score, diag)
        for d in range(T // lanes):
            sl = slice(d * lanes, (d + 1) * lanes)
            a_scr[sl, sl] += diag[sl, :]
        outs.append(dot(a_scr[...].astype(BF16), v16[:, cs]))
        after_head(h)
    return q_block, k_dec, jnp.exp(b_end), jnp.concatenate(outs, axis=1)


def _hgrn_kernel(q_ref, kd_ref, v_ref, gb_ref, oi_ref, dec_ref, on_ref, ob_ref, s_ref, st_scr):
    i = pl.program_id(1)

    @pl.when(i == 0)
    def _():
        st_scr[...] = jnp.zeros_like(st_scr)

    for h in range(B_HEADS):
        cs = slice(h * B_KDIM, (h + 1) * B_KDIM)
        st = st_scr[h]
        o = oi_ref[0, :, cs] + lax.dot_general(q_ref[0, :, cs], st.astype(BF16), (((1,), (1,)), ((), ())),
                                               preferred_element_type=F32)
        ob_ref[0, :, cs] = (_rms(o, on_ref[:, cs]) * gb_ref[0, :, cs].astype(F32)).astype(BF16)
        upd = lax.dot_general(v_ref[0, :, cs], kd_ref[0, :, cs], (((0,), (0,)), ((), ())),
                              preferred_element_type=F32)
        st_scr[h] = st * dec_ref[0, 0, :, cs] + upd

    @pl.when(i == pl.num_programs(1) - 1)
    def _():
        for h in range(B_HEADS):
            s_ref[0, h] = st_scr[h].T


def _hgrn_prompt(q_block, k_dec, v, gb, o_intra, decay, onorm, *, T):
    G, R, W = q_block.shape
    tok = pl.BlockSpec((1, T, W), lambda g, i: (g, i, 0))
    return pl.pallas_call(
        _hgrn_kernel,
        grid=(G, R // T),
        in_specs=[tok, tok, tok, tok, tok, pl.BlockSpec((1, 1, 1, W), lambda g, i: (g, i, 0, 0)),
                  _const_spec((1, W))],
        out_specs=[tok, pl.BlockSpec((1, B_HEADS, B_KDIM, B_VDIM), lambda g, i: (g, 0, 0, 0))],
        out_shape=[jax.ShapeDtypeStruct((G, R, W), BF16),
                   jax.ShapeDtypeStruct((G, B_HEADS, B_KDIM, B_VDIM), F32)],
        scratch_shapes=[pltpu.VMEM((B_HEADS, B_VDIM, B_KDIM), F32)],
        compiler_params=_params(("parallel", "arbitrary")),
        name="hgrn_prompt",
    )(q_block, k_dec, v, gb, o_intra, decay, onorm)


def _hgrn_step_kernel(q_ref, lf_ref, v_ref, gb_ref, on_ref, s0_ref, ob_ref, s1_ref, *, n_seq):
    eye = (lax.broadcasted_iota(jnp.int32, (B_KDIM, B_KDIM), 0)
           == lax.broadcasted_iota(jnp.int32, (B_KDIM, B_KDIM), 1))
    f_all = jnp.exp(lf_ref[...])
    for s in range(n_seq):
        row = slice(s, s + 1)
        for h in range(B_HEADS):
            cs = slice(h * B_KDIM, (h + 1) * B_KDIM)
            f_col = jnp.sum(jnp.where(eye, jnp.broadcast_to(f_all[row, cs], (B_KDIM, B_KDIM)), 0.0),
                            axis=1, keepdims=True)
            s_new = f_col * s0_ref[0, s, h] + (1.0 - f_col) * v_ref[row, cs]
            s1_ref[0, s, h] = s_new
            q8 = jnp.broadcast_to(q_ref[row, cs], (8, B_KDIM)).astype(BF16)
            o = jnp.dot(q8, s_new.astype(BF16), preferred_element_type=F32)[0:1, :]
            ob_ref[row, cs] = _rms(o, on_ref[:, cs]) * gb_ref[row, cs]


def _hgrn_sample(q, lf, v, gb, onorm, state, layer, *, n_seq):
    n, W = q.shape
    rows = pl.BlockSpec((n_seq, W), lambda b: (b, 0))
    st = pl.BlockSpec((1, n_seq, B_HEADS, B_KDIM, B_VDIM), lambda b: (layer, b, 0, 0, 0))
    st_out = pl.BlockSpec((1, n_seq, B_HEADS, B_KDIM, B_VDIM), lambda b: (0, b, 0, 0, 0))
    ob, s1 = pl.pallas_call(
        functools.partial(_hgrn_step_kernel, n_seq=n_seq),
        grid=(n // n_seq,),
        in_specs=[rows, rows, rows, rows, pl.BlockSpec((1, W), lambda b: (0, 0)), st],
        out_specs=[rows, st_out],
        out_shape=[jax.ShapeDtypeStruct((n, W), F32),
                   jax.ShapeDtypeStruct((1, n, B_HEADS, B_KDIM, B_VDIM), F32)],
        compiler_params=_params(("parallel",)),
        name="hgrn_sample",
    )(q, lf, v, gb, onorm, state)
    return ob, s1[0]


def _attn_kernel(bias_ref, qt_ref, k_ref, vt_ref, o_ref, z_scr, x_scr, acc_scr, *, tq, nh):
    hp = pl.program_id(1)
    i = pl.program_id(2)
    dh = C_HEAD_DIM
    pad = 16
    r_i = lax.broadcasted_iota(jnp.int32, (tq + pad, tq), 0)
    c_i = lax.broadcasted_iota(jnp.int32, (tq + pad, tq), 1)
    suffix = jnp.where((c_i > r_i) | (r_i >= tq), 1.0, 0.0).astype(BF16)
    keep = (lax.broadcasted_iota(jnp.int32, (tq, tq), 0)
            < lax.broadcasted_iota(jnp.int32, (tq, tq), 1))
    heads = range(nh)

    def logits(j):
        r0 = pl.multiple_of(j * tq, tq)
        for hh in heads:
            kb = k_ref[0, hh, pl.ds(r0, tq), :]
            z = jnp.dot(kb, qt_ref[0, hh * dh:(hh + 1) * dh, :], preferred_element_type=F32)
            z_scr[hh] = z + bias_ref[nh * hp + hh] * LOG2E

    def exponents(later, diag):
        out = []
        for hh in heads:
            z = z_scr[hh]
            sp = jnp.maximum(z, 0.0) + jnp.log2(1.0 + jnp.exp2(-jnp.abs(z)))
            if diag:
                sp = jnp.where(keep, sp, 0.0)
            sums = jnp.dot(suffix, sp.astype(BF16), preferred_element_type=F32)
            x = z - sp - sums[0:tq] - later[hh]
            x_scr[hh] = jnp.where(keep, x, MASKED_EXPONENT) if diag else x
            out.append(later[hh] + sums[tq:tq + 1])
        return out

    def weigh(j):
        r0 = pl.multiple_of(j * tq, tq)
        for hh in heads:
            vb = vt_ref[0, hh * dh:(hh + 1) * dh, pl.ds(r0, tq)]
            acc_scr[hh] += jnp.dot(vb, jnp.exp2(x_scr[hh]).astype(BF16), preferred_element_type=F32)

    acc_scr[...] = jnp.zeros_like(acc_scr)
    logits(i)
    later = exponents([jnp.zeros((1, tq), F32) for hh in heads], True)
    logits(jnp.maximum(i - 1, 0))

    def body(m, later):
        weigh(i - m + 2)
        later = exponents(later, False)
        logits(i - m)
        return later

    later = lax.fori_loop(2, i + 1, body, later)

    @pl.when(i >= 1)
    def _():
        weigh(1)
        exponents(later, False)

    weigh(0)
    o_ref[0] = jnp.concatenate([acc_scr[hh].T for hh in heads], axis=1).astype(BF16)


def _attn_prompt(bias, q_t, k, v_t, *, tq, nh):
    G, H, R, Dh = k.shape
    return pl.pallas_call(
        functools.partial(_attn_kernel, tq=tq, nh=nh),
        grid=(G, H // nh, R // tq),
        in_specs=[pl.BlockSpec(memory_space=pltpu.SMEM),
                  pl.BlockSpec((1, nh * Dh, tq), lambda g, hp, i: (g, hp, i)),
                  pl.BlockSpec((1, nh, R, Dh), lambda g, hp, i: (g, hp, 0, 0)),
                  pl.BlockSpec((1, nh * Dh, R), lambda g, hp, i: (g, hp, 0))],
        out_specs=pl.BlockSpec((1, tq, nh * Dh), lambda g, hp, i: (g, i, hp)),
        out_shape=jax.ShapeDtypeStruct((G, R, H * Dh), BF16),
        scratch_shapes=[pltpu.VMEM((nh, tq, tq), F32), pltpu.VMEM((nh, tq, tq), F32),
                        pltpu.VMEM((nh, Dh, tq), F32)],
        compiler_params=_params(("parallel", "parallel", "arbitrary")),
        name="attn_prompt",
    )(bias, q_t, k, v_t)


def _attn_seq_kernel(pt_ref, qt_ref, bias_ref, *refs, n_pages):
    k_refs = refs[:n_pages]
    v_refs = refs[n_pages:2 * n_pages]
    o_ref, qb_scr, z_scr, w_scr = refs[2 * n_pages:]
    b = pl.program_id(0)
    ns = qt_ref.shape[1]
    sub = 8
    groups = C_HEAD_DIM // sub

    @pl.when(b == 0)
    def _():
        o_ref[...] = jnp.zeros_like(o_ref)

    mine = lax.broadcasted_iota(jnp.int32, (sub, ns), 1) == b
    for r in range(C_WIDTH // sub):
        rows = slice(r * sub, (r + 1) * sub)
        col = jnp.sum(jnp.where(mine, qt_ref[rows, :], 0.0), axis=1, keepdims=True)
        qb_scr[rows, :] = jnp.broadcast_to(col, (sub, PAGE_SIZE))

    for p in range(n_pages):
        for h in range(C_HEADS):
            part = None
            for r in range(groups):
                rows = slice(h * C_HEAD_DIM + r * sub, h * C_HEAD_DIM + (r + 1) * sub)
                t = k_refs[p][0, 0, rows, :] * qb_scr[rows, :]
                part = t if part is None else part + t
            z_scr[h:h + 1, p * PAGE_SIZE:(p + 1) * PAGE_SIZE] = jnp.sum(part, axis=0, keepdims=True)

    z = z_scr[...] + bias_ref[...]
    sp = _softplus(z)
    log_beta = z - sp
    r_i = lax.broadcasted_iota(jnp.int32, (PAGE_SIZE, PAGE_SIZE), 0)
    c_i = lax.broadcasted_iota(jnp.int32, (PAGE_SIZE, PAGE_SIZE), 1)
    upper = jnp.where(r_i > c_i, 1.0, 0.0).astype(BF16)
    later = jnp.zeros((C_HEADS, 1), F32)
    for p in reversed(range(n_pages)):
        ps = slice(p * PAGE_SIZE, (p + 1) * PAGE_SIZE)
        sp_hi = sp[:, ps].astype(BF16)
        sp_lo = (sp[:, ps] - sp_hi.astype(F32)).astype(BF16)
        within = (jnp.dot(sp_hi, upper, preferred_element_type=F32)
                  + jnp.dot(sp_lo, upper, preferred_element_type=F32))
        w_scr[:, ps] = jnp.exp(log_beta[:, ps] - within - later)
        later = later + jnp.sum(sp[:, ps], axis=1, keepdims=True)

    for r in range(C_WIDTH // sub):
        rows = slice(r * sub, (r + 1) * sub)
        h = r // groups
        acc = None
        for p in range(n_pages):
            wb = jnp.broadcast_to(w_scr[h:h + 1, p * PAGE_SIZE:(p + 1) * PAGE_SIZE], (sub, PAGE_SIZE))
            t = v_refs[p][0, 0, rows, :] * wb
            acc = t if acc is None else acc + t
        col = jnp.sum(acc, axis=1, keepdims=True)
        o_ref[rows, :] = jnp.where(mine, col, o_ref[rows, :])


def _attn_sample(page_table, q_t, bias, cache_k, cache_v, layer):
    n, n_pages = page_table.shape
    depth, n_pool = cache_k.shape[:2]
    ck = jnp.transpose(cache_k, (0, 1, 3, 4, 2)).reshape(depth, n_pool, C_WIDTH, PAGE_SIZE)
    cv = jnp.transpose(cache_v, (0, 1, 3, 4, 2)).reshape(depth, n_pool, C_WIDTH, PAGE_SIZE)

    def page(j):
        return pl.BlockSpec((1, 1, C_WIDTH, PAGE_SIZE), lambda b, pt: (layer, pt[b, j], 0, 0))

    whole = lambda shape: pl.BlockSpec(shape, lambda b, pt: (0,) * len(shape))
    pages = [page(j) for j in range(n_pages)]
    return pl.pallas_call(
        functools.partial(_attn_seq_kernel, n_pages=n_pages),
        grid_spec=pltpu.PrefetchScalarGridSpec(
            num_scalar_prefetch=1,
            grid=(n,),
            in_specs=[whole((C_WIDTH, n)), whole((C_HEADS, 1))] + pages + pages,
            out_specs=whole((C_WIDTH, n)),
            scratch_shapes=[pltpu.VMEM((C_WIDTH, PAGE_SIZE), F32),
                            pltpu.VMEM((C_HEADS, n_pages * PAGE_SIZE), F32),
                            pltpu.VMEM((C_HEADS, n_pages * PAGE_SIZE), F32)]),
        out_shape=jax.ShapeDtypeStruct((C_WIDTH, n), F32),
        compiler_params=_params(("arbitrary",)),
        name="attn_sample",
    )(page_table, q_t, bias.reshape(C_HEADS, 1), *([ck] * n_pages), *([cv] * n_pages))


def _ffn_kernel(x_ref, oa_ref, ob_ref, oc_ref, gate_ref, g1_ref, sh2_ref, sc2_ref, g2_ref,
                wa_ref, wb_ref, wc_ref, wo_ref, n2_ref, w1_ref, w2_ref, fn_ref, y_ref, *, final, ff_chunk,
                oc_channel_major):
    D = D_MODEL
    dot = functools.partial(jnp.dot, preferred_element_type=F32)
    oc = oc_ref[0].T if oc_channel_major else oc_ref[0]
    merged = (gate_ref[0, :, 0:D].astype(F32) * dot(oa_ref[0].astype(BF16), wa_ref[...])
              + gate_ref[0, :, D:2 * D].astype(F32) * dot(ob_ref[0].astype(BF16), wb_ref[...])
              + gate_ref[0, :, 2 * D:3 * D].astype(F32) * dot(oc.astype(BF16), wc_ref[...]))
    x1 = x_ref[0] + g1_ref[0] * dot(merged.astype(BF16), wo_ref[...])
    h2 = (_rms(x1, n2_ref[...]) * (1.0 + sc2_ref[0]) + sh2_ref[0]).astype(BF16)
    y = jnp.zeros_like(x1)
    for c in range(D_FF // ff_chunk):
        cs = slice(c * ff_chunk, (c + 1) * ff_chunk)
        a = jnp.maximum(dot(h2, w1_ref[:, cs]), 0.0)
        y = y + dot((a * a).astype(BF16), w2_ref[cs, :])
    x2 = x1 + g2_ref[0] * y
    y_ref[0] = _rms(x2, fn_ref[...]) if final else x2


def _ffn(x, oa, ob, oc, gate, mod, wa, wb, wc, wo, n2, w1, w2, fn, *, sample, tm, final):
    G, R, D = x.shape
    tok = lambda w: pl.BlockSpec((1, tm, w), lambda g, i: (g, i, 0))
    return pl.pallas_call(
        functools.partial(_ffn_kernel, final=final, ff_chunk=1024, oc_channel_major=sample),
        grid=(G, R // tm),
        in_specs=[tok(D), tok(512), tok(512),
                  pl.BlockSpec((1, 512, tm), lambda g, i: (g, 0, i)) if sample else tok(512), tok(3 * D),
                  _mod_spec(sample, tm, 2), _mod_spec(sample, tm, 3), _mod_spec(sample, tm, 4),
                  _mod_spec(sample, tm, 5),
                  _const_spec((512, D)), _const_spec((512, D)), _const_spec((512, D)), _const_spec((D, D)),
                  _const_spec((1, D)), _const_spec((D, D_FF)), _const_spec((D_FF, D)), _const_spec((1, D))],
        out_specs=tok(D),
        out_shape=jax.ShapeDtypeStruct((G, R, D), F32),
        compiler_params=_params(("parallel", "parallel")),
        name="ffn_sample" if sample else "ffn_prompt",
    )(x, oa, ob, oc, gate, mod, mod, mod, mod, wa, wb, wc, wo, n2, w1, w2, fn)


def _token_major(x):
    G, _, R = x.shape
    return jnp.transpose(x.reshape(G, C_HEADS, C_HEAD_DIM, R), (0, 3, 1, 2))


def kernel(x_prompt, x_sample, c_prompt, c_sample, cache_k, cache_v, state_hgrn, page_table, w_ada, b_ada,
           norm1_w, norm2_w, w_in, gmlp_vnorm_w, gmlp_w_s, gmlp_b_s, sb_bias, hgrn_lb_logits, hgrn_onorm_w,
           w_branch_a, w_branch_b, w_branch_c, w_out, w_ff1, w_ff2, final_norm_w):
    depth = w_in.shape[0]
    nb, seq, d = x_prompt.shape
    ns = x_sample.shape[0]

    lb_cum = jnp.cumsum(jax.nn.softmax(hgrn_lb_logits.astype(F32), axis=0), axis=0)
    lower_bounds = lb_cum - lb_cum[0:1]

    n_c = nb + ns
    pad = (-n_c) % 8
    c_all = jnp.concatenate([c_prompt, c_sample, jnp.zeros((pad, d), F32)], axis=0)
    ada = _ada(c_all, w_ada, b_ada)

    bf = lambda w: w.astype(BF16)
    w_in16, wa16, wb16, wc16 = bf(w_in), bf(w_branch_a), bf(w_branch_b), bf(w_branch_c)
    wo16, w116, w216 = bf(w_out), bf(w_ff1), bf(w_ff2)
    fn = final_norm_w.reshape(1, d)
    gd = A_WIDTH // A_GROUPS

    yp = x_prompt
    ys = x_sample.reshape(1, ns, d)
    kp_l, vp_l, sp_l, ks_l, vs_l, ss_l, gv_l = [], [], [], [], [], [], []
    for l in range(depth):
        final = l == depth - 1
        mod_p = ada[l, :nb].reshape(nb, 1, 6 * d)
        mod_s = ada[l, nb:n_c].reshape(1, ns, 6 * d)
        n1 = norm1_w[l].reshape(1, d)
        n2 = norm2_w[l].reshape(1, d)
        vn = gmlp_vnorm_w[l].reshape(1, A_WIDTH)
        lb = lower_bounds[l].reshape(1, 512)
        onorm = hgrn_onorm_w[l].reshape(1, B_WIDTH)
        b_tile = jnp.broadcast_to(gmlp_b_s[l][:, :, None], (A_GROUPS, A_CHUNK, gd))
        first = jnp.stack([jnp.repeat(gmlp_w_s[l, :, 0, 0], gd), jnp.repeat(gmlp_b_s[l, :, 0], gd)])

        oa, qb, kd, vb, gb, oi, dec, qt, kh, vt, k_c, v_c, gate = _inproj(
            yp, mod_p, n1, w_in16[l], vn, gmlp_w_s[l], b_tile, lb, sample=False, tm=HGRN_BLOCK)
        ob, s_fin = _hgrn_prompt(qb, kd, vb, gb, oi, dec, onorm, T=HGRN_BLOCK)
        oc = _attn_prompt(sb_bias[l], qt, kh, vt, tq=ATT_TQ, nh=ATT_HEADS)
        yp = _ffn(yp, oa, ob, oc, gate, mod_p, wa16[l], wb16[l], wc16[l], wo16[l], n2, w116[l], w216[l], fn,
                  sample=False, tm=256, final=final)
        kp_l.append(_token_major(k_c))
        vp_l.append(_token_major(v_c))
        sp_l.append(s_fin)

        oa, qb, lf, vb, gb, qc, k_c, v_c, gate, va = _inproj(
            ys, mod_s, n1, w_in16[l], vn, first, b_tile[:1, :8], lb, sample=True, tm=ns)
        ob, s_new = _hgrn_sample(qb[0], lf[0], vb[0], gb[0], onorm, state_hgrn, l, n_seq=8)
        oc = _attn_sample(page_table, qc[0], sb_bias[l], cache_k, cache_v, l)
        ys = _ffn(ys, oa, ob[None], oc[None], gate, mod_s, wa16[l], wb16[l], wc16[l], wo16[l],
                  n2, w116[l], w216[l], fn, sample=True, tm=ns, final=final)
        ks_l.append(_token_major(k_c).reshape(ns, 1, C_HEADS, C_HEAD_DIM))
        vs_l.append(_token_major(v_c).reshape(ns, 1, C_HEADS, C_HEAD_DIM))
        ss_l.append(s_new)
        gv_l.append(va.reshape(ns, 1, A_WIDTH))

    return (yp, ys.reshape(ns, 1, d), jnp.stack(kp_l), jnp.stack(vp_l), jnp.stack(sp_l),
            jnp.stack(ks_l), jnp.stack(vs_l), jnp.stack(ss_l), jnp.stack(gv_l))
```

```python
import functools

import jax
import jax.numpy as jnp
from jax import lax
from jax.experimental import pallas as pl
from jax.experimental.pallas import tpu as pltpu

F32 = jnp.float32
BF16 = jnp.bfloat16

D_MODEL = 1024
A_WIDTH = 512
A_GROUPS = 4
A_CHUNK = 128
B_HEADS = 4
B_KDIM = 128
B_VDIM = 128
B_WIDTH = 512
C_HEADS = 8
C_HEAD_DIM = 64
C_WIDTH = 512
D_FF = 4096
PAGE_SIZE = 128
EPS = 1e-6

_SEG = dict(a_u=(0, 512), a_v=(512, 1024), b_q=(1024, 1536), b_f=(1536, 2048), b_i=(2048, 2560),
            b_g=(2560, 3072), c_q=(3072, 3584), c_k=(3584, 4096), c_v=(4096, 4608), gate=(4608, 7680))
IN_COLS = 7680

VMEM_LIMIT = 56 * 1024 * 1024
HGRN_SUB = 16
ATT_TQ = 256
ATT_HEADS = 4
LOG2E = 1.4426950408889634
MASKED_EXPONENT = -1e30


def _rms(x, w):
    return x * lax.rsqrt(jnp.mean(x * x, axis=-1, keepdims=True) + EPS) * w


def _sigmoid(x):
    return 0.5 * jnp.tanh(0.5 * x) + 0.5


def _silu(x):
    return x * _sigmoid(x)


def _softplus(z):
    return jnp.maximum(z, 0.0) + jnp.log(1.0 + jnp.exp(-jnp.abs(z)))


def _const_spec(shape):
    nd = len(shape)
    return pl.BlockSpec(shape, lambda *_: (0,) * nd, pipeline_mode=pl.Buffered(1))


def _params(sem):
    return pltpu.CompilerParams(dimension_semantics=sem, vmem_limit_bytes=VMEM_LIMIT)


def _ada_kernel(c_ref, w_ref, b_ref, o_ref):
    s = _silu(c_ref[...]).astype(BF16)
    o_ref[0] = jnp.dot(s, w_ref[0].astype(BF16), preferred_element_type=F32) + b_ref[0]


def _ada(c_all, w_ada, b_ada):
    depth, d, n = w_ada.shape
    m = c_all.shape[0]
    tn = 1536
    return pl.pallas_call(
        _ada_kernel,
        grid=(depth, n // tn),
        in_specs=[pl.BlockSpec((m, d), lambda l, j: (0, 0)),
                  pl.BlockSpec((1, d, tn), lambda l, j: (l, 0, j)),
                  pl.BlockSpec((1, 1, tn), lambda l, j: (l, 0, j))],
        out_specs=pl.BlockSpec((1, m, tn), lambda l, j: (l, 0, j)),
        out_shape=jax.ShapeDtypeStruct((depth, m, n), F32),
        compiler_params=_params(("arbitrary", "arbitrary")),
        name="ada",
    )(c_all, w_ada, b_ada.reshape(depth, 1, n))


def _inproj_kernel(x_ref, sh_ref, sc_ref, n1_ref, w_ref, vn_ref, ws_ref, bst_ref, lb_ref, *out_refs, tm, sample,
                   n_carried):
    out_refs = out_refs[n_carried:]
    if sample:
        oa_ref, qb_ref, lf_ref, vb_ref, gb_ref, qc_ref, k_ref, v_ref, gate_ref, va_ref = out_refs
    else:
        oa_ref, qb_ref, lf_ref, vb_ref, gb_ref, qc_ref, kh_ref, vt_ref, k_ref, v_ref, gate_ref = out_refs
    act = qb_ref.dtype
    x = x_ref[0]
    h = (_rms(x, n1_ref[...]) * (1.0 + sc_ref[0]) + sh_ref[0]).astype(BF16)

    def seg(name):
        a, b = _SEG[name]
        return jnp.dot(h, w_ref[:, a:b], preferred_element_type=F32)

    u = jax.nn.gelu(seg("a_u"))
    va = _rms(jax.nn.gelu(seg("a_v")), vn_ref[...])
    gd = A_WIDTH // A_GROUPS
    if sample:
        va_ref[0] = va
        oa_ref[0] = u * (va * ws_ref[0:1, :] + ws_ref[1:2, :])
    else:
        row = lax.broadcasted_iota(jnp.int32, (A_CHUNK, A_CHUNK), 0)
        col = lax.broadcasted_iota(jnp.int32, (A_CHUNK, A_CHUNK), 1)
        va16 = va.astype(BF16)
        for g in range(A_GROUPS):
            cs = slice(g * gd, (g + 1) * gd)
            wg = jnp.where(col <= row, ws_ref[g], 0.0).astype(BF16)
            for c in range(tm // A_CHUNK):
                rs = slice(c * A_CHUNK, (c + 1) * A_CHUNK)
                mixed = jnp.dot(wg, va16[rs, cs], preferred_element_type=F32) + bst_ref[g]
                oa_ref[0, rs, cs] = (u[rs, cs] * mixed).astype(BF16)

    qb_ref[0] = _silu(seg("b_q")).astype(act)
    bf = seg("b_f")
    lb = lb_ref[...]
    log_sig = jnp.minimum(bf, 0.0) - jnp.log(1.0 + jnp.exp(-jnp.abs(bf)))
    a = jnp.log(lb)
    b = jnp.log(1.0 - lb) + log_sig
    lf_ref[0] = jnp.maximum(a, b) + jnp.log(1.0 + jnp.exp(-jnp.abs(a - b)))
    vb_ref[0] = seg("b_i").astype(act)
    gb_ref[0] = _silu(seg("b_g")).astype(act)

    cq = seg("c_q") * (C_HEAD_DIM ** -0.5 * (1.0 if sample else LOG2E))
    ck = seg("c_k")
    cv = seg("c_v")
    k_ref[...] = ck.T.reshape(k_ref.shape)
    cv_t = cv.T
    v_ref[...] = cv_t.reshape(v_ref.shape)
    qc_ref[0] = cq.T.astype(qc_ref.dtype)
    if not sample:
        vt_ref[0] = cv_t.astype(BF16)
        for hh in range(C_HEADS):
            kh_ref[0, hh] = ck[:, hh * C_HEAD_DIM:(hh + 1) * C_HEAD_DIM].astype(BF16)

    gate_ref[0] = _sigmoid(seg("gate")).astype(BF16)


def _mod_spec(sample, tm, k):
    if sample:
        return pl.BlockSpec((1, tm, D_MODEL), lambda g, i: (g, i, k))
    return pl.BlockSpec((1, 1, D_MODEL), lambda g, i: (g, 0, k))


def _inproj(x, mod, n1, w_in, vn, w_s, b_s, lb, *, sample, tm, layer=0, depth=1, carried=()):
    G, R, D = x.shape
    tok = lambda w: pl.BlockSpec((1, tm, w), lambda g, i: (g, i, 0))
    hm = pl.BlockSpec((1, C_HEADS, tm, C_HEAD_DIM), lambda g, i: (g, 0, i, 0))
    sd = jax.ShapeDtypeStruct
    act = F32 if sample else BF16
    w512 = lambda dt: sd((G, R, 512), dt)
    out_shape = [w512(act), w512(act), w512(F32), w512(act), w512(act)]
    out_specs = [tok(512)] * 5
    chan = pl.BlockSpec((1, 512, tm), lambda g, i: (g, 0, i))
    if sample:
        out_shape += [sd((G, 512, R), F32)]
        out_specs += [chan]
    else:
        out_shape += [sd((G, 512, R), BF16), sd((G, C_HEADS, R, C_HEAD_DIM), BF16), sd((G, 512, R), BF16)]
        out_specs += [chan, hm, chan]
    kv_index = len(out_shape)
    slab = pl.BlockSpec((1, 1, 512, tm), lambda g, i: (layer, g, 0, i))
    out_shape += [sd((depth, G, 512, R), F32), sd((depth, G, 512, R), F32), sd((G, R, 3 * D_MODEL), BF16)]
    out_specs += [slab, slab, tok(3 * D_MODEL)]
    if sample:
        out_shape += [w512(F32)]
        out_specs += [tok(512)]
    operands = (x, mod, mod, n1, w_in, vn, w_s, b_s, lb)
    return pl.pallas_call(
        functools.partial(_inproj_kernel, tm=tm, sample=sample, n_carried=len(carried)),
        grid=(G, R // tm),
        in_specs=[tok(D), _mod_spec(sample, tm, 0), _mod_spec(sample, tm, 1),
                  _const_spec((1, D)), _const_spec((D, IN_COLS)), _const_spec((1, A_WIDTH)),
                  _const_spec(w_s.shape), _const_spec(b_s.shape), _const_spec((1, 512))]
        + [pl.BlockSpec(memory_space=pl.ANY)] * len(carried),
        out_specs=out_specs,
        out_shape=out_shape,
        input_output_aliases={len(operands) + n: kv_index + n for n in range(len(carried))},
        compiler_params=_params(("parallel", "parallel")),
        name="inproj_sample" if sample else "inproj_prompt",
    )(*operands, *carried)


def _hgrn_kernel(q_ref, lf_ref, v_ref, gb_ref, on_ref, ob_ref, s_ref, st_scr, a_scr, *, T, C):
    i = pl.program_id(1)

    @pl.when(i == 0)
    def _():
        st_scr[...] = jnp.zeros_like(st_scr)

    dot = functools.partial(jnp.dot, preferred_element_type=F32)
    nt = (((1,), (1,)), ((), ()))
    log2c = C.bit_length() - 1
    lanes = B_KDIM
    r_i = lax.broadcasted_iota(jnp.int32, (T, T), 0)
    c_i = lax.broadcasted_iota(jnp.int32, (T, T), 1)
    same_sub = lax.shift_right_logical(r_i, log2c) == lax.shift_right_logical(c_i, log2c)
    tri_block = jnp.where(c_i <= r_i, 1.0, 0.0).astype(BF16)
    tri_sub = jnp.where((c_i <= r_i) & same_sub, 1.0, 0.0).astype(BF16)

    g = lf_ref[0]
    g1 = g.astype(BF16)
    rem = g - g1.astype(F32)
    g2 = rem.astype(BF16)
    g3 = (rem - g2.astype(F32)).astype(BF16)
    b = dot(tri_block, g1) + dot(tri_block, g2) + dot(tri_block, g3)
    bs = dot(tri_sub, g1) + dot(tri_sub, g2) + dot(tri_sub, g3)
    f = jnp.exp(g)
    kk = 1.0 - f
    q = q_ref[0].astype(F32)
    v16 = v_ref[0]
    b_end = b[T - 1:T, :]
    q_block = (q * jnp.exp(b)).astype(BF16)
    q_sub = (q * jnp.exp(bs)).astype(BF16)
    k_dec = (kk * jnp.exp(b_end - b)).astype(BF16)
    decay = jnp.exp(b_end)
    gb = gb_ref[0].astype(F32)

    rl = lax.broadcasted_iota(jnp.int32, (T, lanes), 0) & (lanes - 1)
    cl = lax.broadcasted_iota(jnp.int32, (T, lanes), 1)
    offset = jnp.where(lax.shift_right_logical(rl, log2c) == lax.shift_right_logical(cl, log2c), cl - rl, 1)
    ones = jnp.ones((lanes, lanes), BF16)

    for h in range(B_HEADS):
        cs = slice(h * B_KDIM, (h + 1) * B_KDIM)
        st = st_scr[h]
        o = lax.dot_general(q_block[:, cs], st.astype(BF16), nt, preferred_element_type=F32)

        a_scr[...] = jnp.zeros_like(a_scr)
        for s in range(1, T // C):
            rows = slice(s * C, (s + 1) * C)
            ref_b = b[s * C - 1:s * C, cs]
            k_hat = (kk[0:s * C, cs] * jnp.exp(ref_b - b[0:s * C, cs])).astype(BF16)
            a_scr[rows, 0:s * C] = lax.dot_general(q_sub[rows, cs], k_hat, nt, preferred_element_type=F32)

        qh, fh = q[:, cs], f[:, cs]
        k_decayed = kk[:, cs]
        diag = jnp.zeros((T, lanes), F32)
        for delta in range(C):
            if delta > 0:
                k_decayed = fh * pltpu.roll(k_decayed, 1, axis=0)
            score = dot((qh * k_decayed).astype(BF16), ones)
            diag = jnp.where(offset == -delta, score, diag)
        for d in range(T // lanes):
            sl = slice(d * lanes, (d + 1) * lanes)
            a_scr[sl, sl] += diag[sl, :]

        o = o + dot(a_scr[...].astype(BF16), v16[:, cs])
        ob_ref[0, :, cs] = (_rms(o, on_ref[:, cs]) * gb[:, cs]).astype(BF16)
        upd = lax.dot_general(v16[:, cs], k_dec[:, cs], (((0,), (0,)), ((), ())), preferred_element_type=F32)
        st_scr[h] = st * decay[:, cs] + upd

    @pl.when(i == pl.num_programs(1) - 1)
    def _():
        for h in range(B_HEADS):
            s_ref[0, h] = st_scr[h].T


def _hgrn_prompt(q, lf, v, gb, onorm, *, T):
    G, R, W = q.shape
    tok = pl.BlockSpec((1, T, W), lambda g, i: (g, i, 0))
    return pl.pallas_call(
        functools.partial(_hgrn_kernel, T=T, C=HGRN_SUB),
        grid=(G, R // T),
        in_specs=[tok, tok, tok, tok, _const_spec((1, W))],
        out_specs=[tok, pl.BlockSpec((1, B_HEADS, B_KDIM, B_VDIM), lambda g, i: (g, 0, 0, 0))],
        out_shape=[jax.ShapeDtypeStruct((G, R, W), BF16),
                   jax.ShapeDtypeStruct((G, B_HEADS, B_KDIM, B_VDIM), F32)],
        scratch_shapes=[pltpu.VMEM((B_HEADS, B_VDIM, B_KDIM), F32), pltpu.VMEM((T, T), F32)],
        compiler_params=_params(("parallel", "arbitrary")),
        name="hgrn_prompt",
    )(q, lf, v, gb, onorm)


def _hgrn_step_kernel(q_ref, lf_ref, v_ref, gb_ref, on_ref, s0_ref, *refs, n_seq):
    ob_ref, s1_ref = refs[-2:]
    eye = (lax.broadcasted_iota(jnp.int32, (B_KDIM, B_KDIM), 0)
           == lax.broadcasted_iota(jnp.int32, (B_KDIM, B_KDIM), 1))
    f_all = jnp.exp(lf_ref[...])
    for s in range(n_seq):
        row = slice(s, s + 1)
        for h in range(B_HEADS):
            cs = slice(h * B_KDIM, (h + 1) * B_KDIM)
            f_col = jnp.sum(jnp.where(eye, jnp.broadcast_to(f_all[row, cs], (B_KDIM, B_KDIM)), 0.0),
                            axis=1, keepdims=True)
            s_new = f_col * s0_ref[0, s, h] + (1.0 - f_col) * v_ref[row, cs]
            s1_ref[0, s, h] = s_new
            q8 = jnp.broadcast_to(q_ref[row, cs], (8, B_KDIM)).astype(BF16)
            o = jnp.dot(q8, s_new.astype(BF16), preferred_element_type=F32)[0:1, :]
            ob_ref[row, cs] = _rms(o, on_ref[:, cs]) * gb_ref[row, cs]


def _hgrn_sample(q, lf, v, gb, onorm, state, layer, *, n_seq, carried=()):
    n, W = q.shape
    rows = pl.BlockSpec((n_seq, W), lambda b: (b, 0))
    st = pl.BlockSpec((1, n_seq, B_HEADS, B_KDIM, B_VDIM), lambda b: (layer, b, 0, 0, 0))
    operands = (q, lf, v, gb, onorm, state)
    return pl.pallas_call(
        functools.partial(_hgrn_step_kernel, n_seq=n_seq),
        grid=(n // n_seq,),
        in_specs=[rows, rows, rows, rows, pl.BlockSpec((1, W), lambda b: (0, 0)), st]
        + [pl.BlockSpec(memory_space=pl.ANY)] * len(carried),
        out_specs=[rows, st],
        out_shape=[jax.ShapeDtypeStruct((n, W), F32), jax.ShapeDtypeStruct(state.shape, F32)],
        input_output_aliases={len(operands) + n_: 1 + n_ for n_ in range(len(carried))},
        compiler_params=_params(("parallel",)),
        name="hgrn_sample",
    )(*operands, *carried)


def _attn_kernel(bias_ref, qt_ref, k_ref, vt_ref, o_ref, z_scr, x_scr, acc_scr, *, tq, nh):
    hp = pl.program_id(1)
    i = pl.program_id(2)
    dh = C_HEAD_DIM
    pad = 16
    r_i = lax.broadcasted_iota(jnp.int32, (tq + pad, tq), 0)
    c_i = lax.broadcasted_iota(jnp.int32, (tq + pad, tq), 1)
    suffix = jnp.where((c_i > r_i) | (r_i >= tq), 1.0, 0.0).astype(BF16)
    keep = (lax.broadcasted_iota(jnp.int32, (tq, tq), 0)
            < lax.broadcasted_iota(jnp.int32, (tq, tq), 1))
    heads = range(nh)

    def logits(j):
        r0 = pl.multiple_of(j * tq, tq)
        for hh in heads:
            kb = k_ref[0, hh, pl.ds(r0, tq), :]
            z = jnp.dot(kb, qt_ref[0, hh * dh:(hh + 1) * dh, :], preferred_element_type=F32)
            z_scr[hh] = z + bias_ref[nh * hp + hh] * LOG2E

    def exponents(later, diag):
        out = []
        for hh in heads:
            z = z_scr[hh]
            sp = jnp.maximum(z, 0.0) + jnp.log2(1.0 + jnp.exp2(-jnp.abs(z)))
            if diag:
                sp = jnp.where(keep, sp, 0.0)
            sums = jnp.dot(suffix, sp.astype(BF16), preferred_element_type=F32)
            x = z - sp - sums[0:tq] - later[hh]
            x_scr[hh] = jnp.where(keep, x, MASKED_EXPONENT) if diag else x
            out.append(later[hh] + sums[tq:tq + 1])
        return out

    def weigh(j):
        r0 = pl.multiple_of(j * tq, tq)
        for hh in heads:
            vb = vt_ref[0, hh * dh:(hh + 1) * dh, pl.ds(r0, tq)]
            acc_scr[hh] += jnp.dot(vb, jnp.exp2(x_scr[hh]).astype(BF16), preferred_element_type=F32)

    acc_scr[...] = jnp.zeros_like(acc_scr)
    logits(i)
    later = exponents([jnp.zeros((1, tq), F32) for hh in heads], True)
    logits(jnp.maximum(i - 1, 0))

    def body(m, later):
        weigh(i - m + 2)
        later = exponents(later, False)
        logits(i - m)
        return later

    later = lax.fori_loop(2, i + 1, body, later)

    @pl.when(i >= 1)
    def _():
        weigh(1)
        exponents(later, False)

    weigh(0)
    o_ref[0] = jnp.concatenate([acc_scr[hh].T for hh in heads], axis=1).astype(BF16)


def _attn_prompt(bias, q_t, k, v_t, *, tq, nh):
    G, H, R, Dh = k.shape
    return pl.pallas_call(
        functools.partial(_attn_kernel, tq=tq, nh=nh),
        grid=(G, H // nh, R // tq),
        in_specs=[pl.BlockSpec(memory_space=pltpu.SMEM),
                  pl.BlockSpec((1, nh * Dh, tq), lambda g, hp, i: (g, hp, i)),
                  pl.BlockSpec((1, nh, R, Dh), lambda g, hp, i: (g, hp, 0, 0)),
                  pl.BlockSpec((1, nh * Dh, R), lambda g, hp, i: (g, hp, 0))],
        out_specs=pl.BlockSpec((1, tq, nh * Dh), lambda g, hp, i: (g, i, hp)),
        out_shape=jax.ShapeDtypeStruct((G, R, H * Dh), BF16),
        scratch_shapes=[pltpu.VMEM((nh, tq, tq), F32), pltpu.VMEM((nh, tq, tq), F32),
                        pltpu.VMEM((nh, Dh, tq), F32)],
        compiler_params=_params(("parallel", "parallel", "arbitrary")),
        name="attn_prompt",
    )(bias, q_t, k, v_t)


def _attn_seq_kernel(pt_ref, qt_ref, bias_ref, *refs, n_pages):
    k_refs = refs[:n_pages]
    v_refs = refs[n_pages:2 * n_pages]
    o_ref, qb_scr, z_scr, w_scr = refs[2 * n_pages:]
    b = pl.program_id(0)
    ns = qt_ref.shape[1]
    sub = 8
    groups = C_HEAD_DIM // sub

    @pl.when(b == 0)
    def _():
        o_ref[...] = jnp.zeros_like(o_ref)

    mine = lax.broadcasted_iota(jnp.int32, (sub, ns), 1) == b
    for r in range(C_WIDTH // sub):
        rows = slice(r * sub, (r + 1) * sub)
        col = jnp.sum(jnp.where(mine, qt_ref[rows, :], 0.0), axis=1, keepdims=True)
        qb_scr[rows, :] = jnp.broadcast_to(col, (sub, PAGE_SIZE))

    for p in range(n_pages):
        for h in range(C_HEADS):
            part = None
            for r in range(groups):
                rows = slice(h * C_HEAD_DIM + r * sub, h * C_HEAD_DIM + (r + 1) * sub)
                t = k_refs[p][0, 0, rows, :] * qb_scr[rows, :]
                part = t if part is None else part + t
            z_scr[h:h + 1, p * PAGE_SIZE:(p + 1) * PAGE_SIZE] = jnp.sum(part, axis=0, keepdims=True)

    z = z_scr[...] + bias_ref[...]
    sp = _softplus(z)
    log_beta = z - sp
    r_i = lax.broadcasted_iota(jnp.int32, (PAGE_SIZE, PAGE_SIZE), 0)
    c_i = lax.broadcasted_iota(jnp.int32, (PAGE_SIZE, PAGE_SIZE), 1)
    upper = jnp.where(r_i > c_i, 1.0, 0.0).astype(BF16)
    later = jnp.zeros((C_HEADS, 1), F32)
    for p in reversed(range(n_pages)):
        ps = slice(p * PAGE_SIZE, (p + 1) * PAGE_SIZE)
        sp_hi = sp[:, ps].astype(BF16)
        sp_lo = (sp[:, ps] - sp_hi.astype(F32)).astype(BF16)
        within = (jnp.dot(sp_hi, upper, preferred_element_type=F32)
                  + jnp.dot(sp_lo, upper, preferred_element_type=F32))
        w_scr[:, ps] = jnp.exp(log_beta[:, ps] - within - later)
        later = later + jnp.sum(sp[:, ps], axis=1, keepdims=True)

    for r in range(C_WIDTH // sub):
        rows = slice(r * sub, (r + 1) * sub)
        h = r // groups
        acc = None
        for p in range(n_pages):
            wb = jnp.broadcast_to(w_scr[h:h + 1, p * PAGE_SIZE:(p + 1) * PAGE_SIZE], (sub, PAGE_SIZE))
            t = v_refs[p][0, 0, rows, :] * wb
            acc = t if acc is None else acc + t
        col = jnp.sum(acc, axis=1, keepdims=True)
        o_ref[rows, :] = jnp.where(mine, col, o_ref[rows, :])


def _attn_sample(page_table, q_t, bias, cache_k, cache_v, layer):
    n, n_pages = page_table.shape
    depth, n_pool = cache_k.shape[:2]
    ck = jnp.transpose(cache_k, (0, 1, 3, 4, 2)).reshape(depth, n_pool, C_WIDTH, PAGE_SIZE)
    cv = jnp.transpose(cache_v, (0, 1, 3, 4, 2)).reshape(depth, n_pool, C_WIDTH, PAGE_SIZE)

    def page(j):
        return pl.BlockSpec((1, 1, C_WIDTH, PAGE_SIZE), lambda b, pt: (layer, pt[b, j], 0, 0))

    whole = lambda shape: pl.BlockSpec(shape, lambda b, pt: (0,) * len(shape))
    pages = [page(j) for j in range(n_pages)]
    return pl.pallas_call(
        functools.partial(_attn_seq_kernel, n_pages=n_pages),
        grid_spec=pltpu.PrefetchScalarGridSpec(
            num_scalar_prefetch=1,
            grid=(n,),
            in_specs=[whole((C_WIDTH, n)), whole((C_HEADS, 1))] + pages + pages,
            out_specs=whole((C_WIDTH, n)),
            scratch_shapes=[pltpu.VMEM((C_WIDTH, PAGE_SIZE), F32),
                            pltpu.VMEM((C_HEADS, n_pages * PAGE_SIZE), F32),
                            pltpu.VMEM((C_HEADS, n_pages * PAGE_SIZE), F32)]),
        out_shape=jax.ShapeDtypeStruct((C_WIDTH, n), F32),
        compiler_params=_params(("arbitrary",)),
        name="attn_sample",
    )(page_table, q_t, bias.reshape(C_HEADS, 1), *([ck] * n_pages), *([cv] * n_pages))


def _ffn_kernel(x_ref, oa_ref, ob_ref, oc_ref, gate_ref, g1_ref, sh2_ref, sc2_ref, g2_ref,
                wa_ref, wb_ref, wc_ref, wo_ref, n2_ref, w1_ref, w2_ref, fn_ref, y_ref, *, final, ff_chunk,
                oc_channel_major):
    D = D_MODEL
    dot = functools.partial(jnp.dot, preferred_element_type=F32)
    oc = oc_ref[0].T if oc_channel_major else oc_ref[0]
    merged = (gate_ref[0, :, 0:D].astype(F32) * dot(oa_ref[0].astype(BF16), wa_ref[...])
              + gate_ref[0, :, D:2 * D].astype(F32) * dot(ob_ref[0].astype(BF16), wb_ref[...])
              + gate_ref[0, :, 2 * D:3 * D].astype(F32) * dot(oc.astype(BF16), wc_ref[...]))
    x1 = x_ref[0] + g1_ref[0] * dot(merged.astype(BF16), wo_ref[...])
    h2 = (_rms(x1, n2_ref[...]) * (1.0 + sc2_ref[0]) + sh2_ref[0]).astype(BF16)
    y = jnp.zeros_like(x1)
    for c in range(D_FF // ff_chunk):
        cs = slice(c * ff_chunk, (c + 1) * ff_chunk)
        a = jnp.maximum(dot(h2, w1_ref[:, cs]), 0.0)
        y = y + dot((a * a).astype(BF16), w2_ref[cs, :])
    x2 = x1 + g2_ref[0] * y
    y_ref[0] = _rms(x2, fn_ref[...]) if final else x2


def _ffn(x, oa, ob, oc, gate, mod, wa, wb, wc, wo, n2, w1, w2, fn, *, sample, tm, final):
    G, R, D = x.shape
    tok = lambda w: pl.BlockSpec((1, tm, w), lambda g, i: (g, i, 0))
    return pl.pallas_call(
        functools.partial(_ffn_kernel, final=final, ff_chunk=1024, oc_channel_major=sample),
        grid=(G, R // tm),
        in_specs=[tok(D), tok(512), tok(512),
                  pl.BlockSpec((1, 512, tm), lambda g, i: (g, 0, i)) if sample else tok(512), tok(3 * D),
                  _mod_spec(sample, tm, 2), _mod_spec(sample, tm, 3), _mod_spec(sample, tm, 4),
                  _mod_spec(sample, tm, 5),
                  _const_spec((512, D)), _const_spec((512, D)), _const_spec((512, D)), _const_spec((D, D)),
                  _const_spec((1, D)), _const_spec((D, D_FF)), _const_spec((D_FF, D)), _const_spec((1, D))],
        out_specs=tok(D),
        out_shape=jax.ShapeDtypeStruct((G, R, D), F32),
        compiler_params=_params(("parallel", "parallel")),
        name="ffn_sample" if sample else "ffn_prompt",
    )(x, oa, ob, oc, gate, mod, mod, mod, mod, wa, wb, wc, wo, n2, w1, w2, fn)


def _token_major(x):
    lead, R = x.shape[:-2], x.shape[-1]
    n = len(lead)
    return jnp.transpose(x.reshape(*lead, C_HEADS, C_HEAD_DIM, R), (*range(n), n + 2, n, n + 1))


def kernel(x_prompt, x_sample, c_prompt, c_sample, cache_k, cache_v, state_hgrn, page_table, w_ada, b_ada,
           norm1_w, norm2_w, w_in, gmlp_vnorm_w, gmlp_w_s, gmlp_b_s, sb_bias, hgrn_lb_logits, hgrn_onorm_w,
           w_branch_a, w_branch_b, w_branch_c, w_out, w_ff1, w_ff2, final_norm_w):
    depth = w_in.shape[0]
    nb, seq, d = x_prompt.shape
    ns = x_sample.shape[0]

    lb_cum = jnp.cumsum(jax.nn.softmax(hgrn_lb_logits.astype(F32), axis=0), axis=0)
    lower_bounds = lb_cum - lb_cum[0:1]

    n_c = nb + ns
    pad = (-n_c) % 8
    c_all = jnp.concatenate([c_prompt, c_sample, jnp.zeros((pad, d), F32)], axis=0)
    ada = _ada(c_all, w_ada, b_ada)

    bf = lambda w: w.astype(BF16)
    w_in16, wa16, wb16, wc16 = bf(w_in), bf(w_branch_a), bf(w_branch_b), bf(w_branch_c)
    wo16, w116, w216 = bf(w_out), bf(w_ff1), bf(w_ff2)
    fn = final_norm_w.reshape(1, d)
    gd = A_WIDTH // A_GROUPS

    yp = x_prompt
    ys = x_sample.reshape(1, ns, d)
    sp_l, ks_l, vs_l, gv_l = [], [], [], []
    for l in range(depth):
        final = l == depth - 1
        mod_p = ada[l, :nb].reshape(nb, 1, 6 * d)
        mod_s = ada[l, nb:n_c].reshape(1, ns, 6 * d)
        n1 = norm1_w[l].reshape(1, d)
        n2 = norm2_w[l].reshape(1, d)
        vn = gmlp_vnorm_w[l].reshape(1, A_WIDTH)
        lb = lower_bounds[l].reshape(1, 512)
        onorm = hgrn_onorm_w[l].reshape(1, B_WIDTH)
        b_tile = jnp.broadcast_to(gmlp_b_s[l][:, :, None], (A_GROUPS, A_CHUNK, gd))
        first = jnp.stack([jnp.repeat(gmlp_w_s[l, :, 0, 0], gd), jnp.repeat(gmlp_b_s[l, :, 0], gd)])

        oa, qb, lf, vb, gb, qt, kh, vt, k_p, v_p, gate = _inproj(
            yp, mod_p, n1, w_in16[l], vn, gmlp_w_s[l], b_tile, lb, sample=False, tm=256,
            layer=l, depth=depth, carried=() if l == 0 else (k_p, v_p))
        ob, s_fin = _hgrn_prompt(qb, lf, vb, gb, onorm, T=256)
        oc = _attn_prompt(sb_bias[l], qt, kh, vt, tq=ATT_TQ, nh=ATT_HEADS)
        yp = _ffn(yp, oa, ob, oc, gate, mod_p, wa16[l], wb16[l], wc16[l], wo16[l], n2, w116[l], w216[l], fn,
                  sample=False, tm=256, final=final)
        sp_l.append(s_fin)

        oa, qb, lf, vb, gb, qc, k_c, v_c, gate, va = _inproj(
            ys, mod_s, n1, w_in16[l], vn, first, b_tile[:1, :8], lb, sample=True, tm=ns)
        ob, s_all = _hgrn_sample(qb[0], lf[0], vb[0], gb[0], onorm, state_hgrn, l, n_seq=8,
                                 carried=() if l == 0 else (s_all,))
        oc = _attn_sample(page_table, qc[0], sb_bias[l], cache_k, cache_v, l)
        ys = _ffn(ys, oa, ob[None], oc[None], gate, mod_s, wa16[l], wb16[l], wc16[l], wo16[l],
                  n2, w116[l], w216[l], fn, sample=True, tm=ns, final=final)
        ks_l.append(_token_major(k_c[0]).reshape(ns, 1, C_HEADS, C_HEAD_DIM))
        vs_l.append(_token_major(v_c[0]).reshape(ns, 1, C_HEADS, C_HEAD_DIM))
        gv_l.append(va.reshape(ns, 1, A_WIDTH))

    return (yp, ys.reshape(ns, 1, d), _token_major(k_p), _token_major(v_p), jnp.stack(sp_l),
            jnp.stack(ks_l), jnp.stack(vs_l), s_all, jnp.stack(gv_l))
```

```python
import functools

import jax
import jax.numpy as jnp
from jax import lax
from jax.experimental import pallas as pl
from jax.experimental.pallas import tpu as pltpu

F32 = jnp.float32
BF16 = jnp.bfloat16

D_MODEL = 1024
A_WIDTH = 512
A_GROUPS = 4
A_CHUNK = 128
B_HEADS = 4
B_KDIM = 128
B_VDIM = 128
B_WIDTH = 512
C_HEADS = 8
C_HEAD_DIM = 64
C_WIDTH = 512
D_FF = 4096
PAGE_SIZE = 128
EPS = 1e-6

_SEG = dict(a_u=(0, 512), a_v=(512, 1024), b_q=(1024, 1536), b_f=(1536, 2048), b_i=(2048, 2560),
            b_g=(2560, 3072), c_q=(3072, 3584), c_k=(3584, 4096), c_v=(4096, 4608), gate=(4608, 7680))
IN_COLS = 7680

VMEM_LIMIT = 56 * 1024 * 1024
HGRN_SUB = 16
ATT_TQ = 256
ATT_HEADS = 8
LOG2E = 1.4426950408889634
MASKED_EXPONENT = -1e30


def _rms(x, w):
    return x * lax.rsqrt(jnp.mean(x * x, axis=-1, keepdims=True) + EPS) * w


def _sigmoid(x):
    return 0.5 * jnp.tanh(0.5 * x) + 0.5


def _silu(x):
    return x * _sigmoid(x)


def _softplus(z):
    return jnp.maximum(z, 0.0) + jnp.log(1.0 + jnp.exp(-jnp.abs(z)))


def _const_spec(shape):
    nd = len(shape)
    return pl.BlockSpec(shape, lambda *_: (0,) * nd, pipeline_mode=pl.Buffered(1))


def _params(sem):
    return pltpu.CompilerParams(dimension_semantics=sem, vmem_limit_bytes=VMEM_LIMIT)


def _ada_kernel(c_ref, w_ref, b_ref, o_ref):
    s = _silu(c_ref[...]).astype(BF16)
    o_ref[0] = jnp.dot(s, w_ref[0].astype(BF16), preferred_element_type=F32) + b_ref[0]


def _ada(c_all, w_ada, b_ada):
    depth, d, n = w_ada.shape
    m = c_all.shape[0]
    tn = 1536
    return pl.pallas_call(
        _ada_kernel,
        grid=(depth, n // tn),
        in_specs=[pl.BlockSpec((m, d), lambda l, j: (0, 0)),
                  pl.BlockSpec((1, d, tn), lambda l, j: (l, 0, j)),
                  pl.BlockSpec((1, 1, tn), lambda l, j: (l, 0, j))],
        out_specs=pl.BlockSpec((1, m, tn), lambda l, j: (l, 0, j)),
        out_shape=jax.ShapeDtypeStruct((depth, m, n), F32),
        compiler_params=_params(("arbitrary", "arbitrary")),
        name="ada",
    )(c_all, w_ada, b_ada.reshape(depth, 1, n))


def _inproj_kernel(x_ref, sh_ref, sc_ref, n1_ref, w_ref, vn_ref, ws_ref, bst_ref, lb_ref, *out_refs, tm, sample,
                   n_carried):
    out_refs = out_refs[n_carried:]
    if sample:
        oa_ref, qb_ref, lf_ref, vb_ref, gb_ref, qc_ref, k_ref, v_ref, gate_ref, va_ref = out_refs
    else:
        oa_ref, qb_ref, lf_ref, vb_ref, gb_ref, qc_ref, kh_ref, vt_ref, k_ref, v_ref, gate_ref = out_refs
    act = qb_ref.dtype
    x = x_ref[0]
    h = (_rms(x, n1_ref[...]) * (1.0 + sc_ref[0]) + sh_ref[0]).astype(BF16)

    def seg(name):
        a, b = _SEG[name]
        return jnp.dot(h, w_ref[:, a:b], preferred_element_type=F32)

    u = jax.nn.gelu(seg("a_u"))
    va = _rms(jax.nn.gelu(seg("a_v")), vn_ref[...])
    gd = A_WIDTH // A_GROUPS
    if sample:
        va_ref[0] = va
        oa_ref[0] = u * (va * ws_ref[0:1, :] + ws_ref[1:2, :])
    else:
        row = lax.broadcasted_iota(jnp.int32, (A_CHUNK, A_CHUNK), 0)
        col = lax.broadcasted_iota(jnp.int32, (A_CHUNK, A_CHUNK), 1)
        va16 = va.astype(BF16)
        for g in range(A_GROUPS):
            cs = slice(g * gd, (g + 1) * gd)
            wg = jnp.where(col <= row, ws_ref[g], 0.0).astype(BF16)
            for c in range(tm // A_CHUNK):
                rs = slice(c * A_CHUNK, (c + 1) * A_CHUNK)
                mixed = jnp.dot(wg, va16[rs, cs], preferred_element_type=F32) + bst_ref[g]
                oa_ref[0, rs, cs] = (u[rs, cs] * mixed).astype(BF16)

    qb_ref[0] = _silu(seg("b_q")).astype(act)
    bf = seg("b_f")
    lb = lb_ref[...]
    log_sig = jnp.minimum(bf, 0.0) - jnp.log(1.0 + jnp.exp(-jnp.abs(bf)))
    a = jnp.log(lb)
    b = jnp.log(1.0 - lb) + log_sig
    lf_ref[0] = jnp.maximum(a, b) + jnp.log(1.0 + jnp.exp(-jnp.abs(a - b)))
    vb_ref[0] = seg("b_i").astype(act)
    gb_ref[0] = _silu(seg("b_g")).astype(act)

    cq = seg("c_q") * (C_HEAD_DIM ** -0.5 * (1.0 if sample else LOG2E))
    ck = seg("c_k")
    cv = seg("c_v")
    k_ref[...] = ck.T.reshape(k_ref.shape)
    cv_t = cv.T
    v_ref[...] = cv_t.reshape(v_ref.shape)
    qc_ref[0] = cq.T.astype(qc_ref.dtype)
    if not sample:
        vt_ref[0] = cv_t.astype(BF16)
        for hh in range(C_HEADS):
            kh_ref[0, hh] = ck[:, hh * C_HEAD_DIM:(hh + 1) * C_HEAD_DIM].astype(BF16)

    gate_ref[0] = _sigmoid(seg("gate")).astype(BF16)


def _mod_spec(sample, tm, k):
    if sample:
        return pl.BlockSpec((1, tm, D_MODEL), lambda g, i: (g, i, k))
    return pl.BlockSpec((1, 1, D_MODEL), lambda g, i: (g, 0, k))


def _inproj(x, mod, n1, w_in, vn, w_s, b_s, lb, *, sample, tm, layer=0, depth=1, carried=()):
    G, R, D = x.shape
    tok = lambda w: pl.BlockSpec((1, tm, w), lambda g, i: (g, i, 0))
    hm = pl.BlockSpec((1, C_HEADS, tm, C_HEAD_DIM), lambda g, i: (g, 0, i, 0))
    sd = jax.ShapeDtypeStruct
    act = F32 if sample else BF16
    w512 = lambda dt: sd((G, R, 512), dt)
    out_shape = [w512(act), w512(act), w512(F32), w512(act), w512(act)]
    out_specs = [tok(512)] * 5
    chan = pl.BlockSpec((1, 512, tm), lambda g, i: (g, 0, i))
    if sample:
        out_shape += [sd((G, 512, R), F32)]
        out_specs += [chan]
    else:
        out_shape += [sd((G, 512, R), BF16), sd((G, C_HEADS, R, C_HEAD_DIM), BF16), sd((G, 512, R), BF16)]
        out_specs += [chan, hm, chan]
    kv_index = len(out_shape)
    slab = pl.BlockSpec((1, 1, 512, tm), lambda g, i: (layer, g, 0, i))
    out_shape += [sd((depth, G, 512, R), F32), sd((depth, G, 512, R), F32), sd((G, R, 3 * D_MODEL), BF16)]
    out_specs += [slab, slab, tok(3 * D_MODEL)]
    if sample:
        out_shape += [w512(F32)]
        out_specs += [tok(512)]
    operands = (x, mod, mod, n1, w_in, vn, w_s, b_s, lb)
    return pl.pallas_call(
        functools.partial(_inproj_kernel, tm=tm, sample=sample, n_carried=len(carried)),
        grid=(G, R // tm),
        in_specs=[tok(D), _mod_spec(sample, tm, 0), _mod_spec(sample, tm, 1),
                  _const_spec((1, D)), _const_spec((D, IN_COLS)), _const_spec((1, A_WIDTH)),
                  _const_spec(w_s.shape), _const_spec(b_s.shape), _const_spec((1, 512))]
        + [pl.BlockSpec(memory_space=pl.ANY)] * len(carried),
        out_specs=out_specs,
        out_shape=out_shape,
        input_output_aliases={len(operands) + n: kv_index + n for n in range(len(carried))},
        compiler_params=_params(("parallel", "parallel")),
        name="inproj_sample" if sample else "inproj_prompt",
    )(*operands, *carried)


def _hgrn_kernel(q_ref, lf_ref, v_ref, gb_ref, on_ref, ob_ref, s_ref, st_scr, a_scr, *, T, C):
    i = pl.program_id(1)

    @pl.when(i == 0)
    def _():
        st_scr[...] = jnp.zeros_like(st_scr)

    dot = functools.partial(jnp.dot, preferred_element_type=F32)
    nt = (((1,), (1,)), ((), ()))
    log2c = C.bit_length() - 1
    lanes = B_KDIM
    r_i = lax.broadcasted_iota(jnp.int32, (T, T), 0)
    c_i = lax.broadcasted_iota(jnp.int32, (T, T), 1)
    same_sub = lax.shift_right_logical(r_i, log2c) == lax.shift_right_logical(c_i, log2c)
    tri_block = jnp.where(c_i <= r_i, 1.0, 0.0).astype(BF16)
    tri_sub = jnp.where((c_i <= r_i) & same_sub, 1.0, 0.0).astype(BF16)

    g = lf_ref[0]
    g1 = g.astype(BF16)
    rem = g - g1.astype(F32)
    g2 = rem.astype(BF16)
    g3 = (rem - g2.astype(F32)).astype(BF16)
    b = dot(tri_block, g1) + dot(tri_block, g2) + dot(tri_block, g3)
    bs = dot(tri_sub, g1) + dot(tri_sub, g2) + dot(tri_sub, g3)
    f = jnp.exp(g)
    kk = 1.0 - f
    q = q_ref[0].astype(F32)
    v16 = v_ref[0]
    b_end = b[T - 1:T, :]
    q_block = (q * jnp.exp(b)).astype(BF16)
    q_sub = (q * jnp.exp(bs)).astype(BF16)
    k_dec = (kk * jnp.exp(b_end - b)).astype(BF16)
    decay = jnp.exp(b_end)
    gb = gb_ref[0].astype(F32)

    rl = lax.broadcasted_iota(jnp.int32, (T, lanes), 0) & (lanes - 1)
    cl = lax.broadcasted_iota(jnp.int32, (T, lanes), 1)
    offset = jnp.where(lax.shift_right_logical(rl, log2c) == lax.shift_right_logical(cl, log2c), cl - rl, 1)
    ones = jnp.ones((lanes, lanes), BF16)

    for h in range(B_HEADS):
        cs = slice(h * B_KDIM, (h + 1) * B_KDIM)
        st = st_scr[h]
        o = lax.dot_general(q_block[:, cs], st.astype(BF16), nt, preferred_element_type=F32)

        a_scr[...] = jnp.zeros_like(a_scr)
        for s in range(1, T // C):
            rows = slice(s * C, (s + 1) * C)
            ref_b = b[s * C - 1:s * C, cs]
            k_hat = (kk[0:s * C, cs] * jnp.exp(ref_b - b[0:s * C, cs])).astype(BF16)
            a_scr[rows, 0:s * C] = lax.dot_general(q_sub[rows, cs], k_hat, nt, preferred_element_type=F32)

        qh, fh = q[:, cs], f[:, cs]
        k_decayed = kk[:, cs]
        diag = jnp.zeros((T, lanes), F32)
        for delta in range(C):
            if delta > 0:
                k_decayed = fh * pltpu.roll(k_decayed, 1, axis=0)
            score = dot((qh * k_decayed).astype(BF16), ones)
            diag = jnp.where(offset == -delta, score, diag)
        for d in range(T // lanes):
            sl = slice(d * lanes, (d + 1) * lanes)
            a_scr[sl, sl] += diag[sl, :]

        o = o + dot(a_scr[...].astype(BF16), v16[:, cs])
        ob_ref[0, :, cs] = (_rms(o, on_ref[:, cs]) * gb[:, cs]).astype(BF16)
        upd = lax.dot_general(v16[:, cs], k_dec[:, cs], (((0,), (0,)), ((), ())), preferred_element_type=F32)
        st_scr[h] = st * decay[:, cs] + upd

    @pl.when(i == pl.num_programs(1) - 1)
    def _():
        for h in range(B_HEADS):
            s_ref[0, h] = st_scr[h].T


def _hgrn_prompt(q, lf, v, gb, onorm, *, T):
    G, R, W = q.shape
    tok = pl.BlockSpec((1, T, W), lambda g, i: (g, i, 0))
    return pl.pallas_call(
        functools.partial(_hgrn_kernel, T=T, C=HGRN_SUB),
        grid=(G, R // T),
        in_specs=[tok, tok, tok, tok, _const_spec((1, W))],
        out_specs=[tok, pl.BlockSpec((1, B_HEADS, B_KDIM, B_VDIM), lambda g, i: (g, 0, 0, 0))],
        out_shape=[jax.ShapeDtypeStruct((G, R, W), BF16),
                   jax.ShapeDtypeStruct((G, B_HEADS, B_KDIM, B_VDIM), F32)],
        scratch_shapes=[pltpu.VMEM((B_HEADS, B_VDIM, B_KDIM), F32), pltpu.VMEM((T, T), F32)],
        compiler_params=_params(("parallel", "arbitrary")),
        name="hgrn_prompt",
    )(q, lf, v, gb, onorm)


def _hgrn_step_kernel(q_ref, lf_ref, v_ref, gb_ref, on_ref, s0_ref, *refs, n_seq):
    ob_ref, s1_ref = refs[-2:]
    eye = (lax.broadcasted_iota(jnp.int32, (B_KDIM, B_KDIM), 0)
           == lax.broadcasted_iota(jnp.int32, (B_KDIM, B_KDIM), 1))
    f_all = jnp.exp(lf_ref[...])
    for s in range(n_seq):
        row = slice(s, s + 1)
        for h in range(B_HEADS):
            cs = slice(h * B_KDIM, (h + 1) * B_KDIM)
            f_col = jnp.sum(jnp.where(eye, jnp.broadcast_to(f_all[row, cs], (B_KDIM, B_KDIM)), 0.0),
                            axis=1, keepdims=True)
            s_new = f_col * s0_ref[0, s, h] + (1.0 - f_col) * v_ref[row, cs]
            s1_ref[0, s, h] = s_new
            q8 = jnp.broadcast_to(q_ref[row, cs], (8, B_KDIM)).astype(BF16)
            o = jnp.dot(q8, s_new.astype(BF16), preferred_element_type=F32)[0:1, :]
            ob_ref[row, cs] = _rms(o, on_ref[:, cs]) * gb_ref[row, cs]


def _hgrn_sample(q, lf, v, gb, onorm, state, layer, *, n_seq, carried=()):
    n, W = q.shape
    rows = pl.BlockSpec((n_seq, W), lambda b: (b, 0))
    st = pl.BlockSpec((1, n_seq, B_HEADS, B_KDIM, B_VDIM), lambda b: (layer, b, 0, 0, 0))
    operands = (q, lf, v, gb, onorm, state)
    return pl.pallas_call(
        functools.partial(_hgrn_step_kernel, n_seq=n_seq),
        grid=(n // n_seq,),
        in_specs=[rows, rows, rows, rows, pl.BlockSpec((1, W), lambda b: (0, 0)), st]
        + [pl.BlockSpec(memory_space=pl.ANY)] * len(carried),
        out_specs=[rows, st],
        out_shape=[jax.ShapeDtypeStruct((n, W), F32), jax.ShapeDtypeStruct(state.shape, F32)],
        input_output_aliases={len(operands) + n_: 1 + n_ for n_ in range(len(carried))},
        compiler_params=_params(("parallel",)),
        name="hgrn_sample",
    )(*operands, *carried)


def _attn_kernel(bias_ref, qt_ref, k_ref, vt_ref, o_ref, z_scr, x_scr, acc_scr, *, tq, nh):
    hp = pl.program_id(1)
    i = pl.program_id(2)
    dh = C_HEAD_DIM
    pad = 16
    r_i = lax.broadcasted_iota(jnp.int32, (tq + pad, tq), 0)
    c_i = lax.broadcasted_iota(jnp.int32, (tq + pad, tq), 1)
    suffix = jnp.where((c_i > r_i) | (r_i >= tq), 1.0, 0.0).astype(BF16)
    keep = (lax.broadcasted_iota(jnp.int32, (tq, tq), 0)
            < lax.broadcasted_iota(jnp.int32, (tq, tq), 1))
    heads = range(nh)

    def logits(j):
        r0 = pl.multiple_of(j * tq, tq)
        for hh in heads:
            kb = k_ref[0, hh, pl.ds(r0, tq), :]
            z = jnp.dot(kb, qt_ref[0, hh * dh:(hh + 1) * dh, :], preferred_element_type=F32)
            z_scr[hh] = z + bias_ref[nh * hp + hh] * LOG2E

    def exponents(later, diag):
        out = []
        for hh in heads:
            z = z_scr[hh]
            sp = jnp.maximum(z, 0.0) + jnp.log2(1.0 + jnp.exp2(-jnp.abs(z)))
            if diag:
                sp = jnp.where(keep, sp, 0.0)
            sums = jnp.dot(suffix, sp.astype(BF16), preferred_element_type=F32)
            x = z - sp - sums[0:tq] - later[hh]
            x_scr[hh] = jnp.where(keep, x, MASKED_EXPONENT) if diag else x
            out.append(later[hh] + sums[tq:tq + 1])
        return out

    def weigh(j):
        r0 = pl.multiple_of(j * tq, tq)
        for hh in heads:
            vb = vt_ref[0, hh * dh:(hh + 1) * dh, pl.ds(r0, tq)]
            acc_scr[hh] += jnp.dot(vb, jnp.exp2(x_scr[hh]).astype(BF16), preferred_element_type=F32)

    acc_scr[...] = jnp.zeros_like(acc_scr)
    logits(i)
    later = exponents([jnp.zeros((1, tq), F32) for hh in heads], True)
    logits(jnp.maximum(i - 1, 0))

    def body(m, later):
        weigh(i - m + 2)
        later = exponents(later, False)
        logits(i - m)
        return later

    later = lax.fori_loop(2, i + 1, body, later)

    @pl.when(i >= 1)
    def _():
        weigh(1)
        exponents(later, False)

    weigh(0)
    o_ref[0] = jnp.concatenate([acc_scr[hh].T for hh in heads], axis=1).astype(BF16)


def _attn_prompt(bias, q_t, k, v_t, *, tq, nh):
    G, H, R, Dh = k.shape
    return pl.pallas_call(
        functools.partial(_attn_kernel, tq=tq, nh=nh),
        grid=(G, H // nh, R // tq),
        in_specs=[pl.BlockSpec(memory_space=pltpu.SMEM),
                  pl.BlockSpec((1, nh * Dh, tq), lambda g, hp, i: (g, hp, i)),
                  pl.BlockSpec((1, nh, R, Dh), lambda g, hp, i: (g, hp, 0, 0)),
                  pl.BlockSpec((1, nh * Dh, R), lambda g, hp, i: (g, hp, 0))],
        out_specs=pl.BlockSpec((1, tq, nh * Dh), lambda g, hp, i: (g, i, hp)),
        out_shape=jax.ShapeDtypeStruct((G, R, H * Dh), BF16),
        scratch_shapes=[pltpu.VMEM((nh, tq, tq), F32), pltpu.VMEM((nh, tq, tq), F32),
                        pltpu.VMEM((nh, Dh, tq), F32)],
        compiler_params=_params(("parallel", "parallel", "arbitrary")),
        name="attn_prompt",
    )(bias, q_t, k, v_t)


def _attn_seq_kernel(pt_ref, qt_ref, bias_ref, *refs, n_pages):
    k_refs = refs[:n_pages]
    v_refs = refs[n_pages:2 * n_pages]
    o_ref, qb_scr, z_scr, w_scr = refs[2 * n_pages:]
    b = pl.program_id(0)
    ns = qt_ref.shape[1]
    sub = 8
    groups = C_HEAD_DIM // sub

    @pl.when(b == 0)
    def _():
        o_ref[...] = jnp.zeros_like(o_ref)

    mine = lax.broadcasted_iota(jnp.int32, (sub, ns), 1) == b
    for r in range(C_WIDTH // sub):
        rows = slice(r * sub, (r + 1) * sub)
        col = jnp.sum(jnp.where(mine, qt_ref[rows, :], 0.0), axis=1, keepdims=True)
        qb_scr[rows, :] = jnp.broadcast_to(col, (sub, PAGE_SIZE))

    for p in range(n_pages):
        for h in range(C_HEADS):
            part = None
            for r in range(groups):
                rows = slice(h * C_HEAD_DIM + r * sub, h * C_HEAD_DIM + (r + 1) * sub)
                t = k_refs[p][0, 0, rows, :] * qb_scr[rows, :]
                part = t if part is None else part + t
            z_scr[h:h + 1, p * PAGE_SIZE:(p + 1) * PAGE_SIZE] = jnp.sum(part, axis=0, keepdims=True)

    z = z_scr[...] + bias_ref[...]
    sp = _softplus(z)
    log_beta = z - sp
    r_i = lax.broadcasted_iota(jnp.int32, (PAGE_SIZE, PAGE_SIZE), 0)
    c_i = lax.broadcasted_iota(jnp.int32, (PAGE_SIZE, PAGE_SIZE), 1)
    upper = jnp.where(r_i > c_i, 1.0, 0.0).astype(BF16)
    later = jnp.zeros((C_HEADS, 1), F32)
    for p in reversed(range(n_pages)):
        ps = slice(p * PAGE_SIZE, (p + 1) * PAGE_SIZE)
        sp_hi = sp[:, ps].astype(BF16)
        sp_lo = (sp[:, ps] - sp_hi.astype(F32)).astype(BF16)
        within = (jnp.dot(sp_hi, upper, preferred_element_type=F32)
                  + jnp.dot(sp_lo, upper, preferred_element_type=F32))
        w_scr[:, ps] = jnp.exp(log_beta[:, ps] - within - later)
        later = later + jnp.sum(sp[:, ps], axis=1, keepdims=True)

    for r in range(C_WIDTH // sub):
        rows = slice(r * sub, (r + 1) * sub)
        h = r // groups
        acc = None
        for p in range(n_pages):
            wb = jnp.broadcast_to(w_scr[h:h + 1, p * PAGE_SIZE:(p + 1) * PAGE_SIZE], (sub, PAGE_SIZE))
            t = v_refs[p][0, 0, rows, :] * wb
            acc = t if acc is None else acc + t
        col = jnp.sum(acc, axis=1, keepdims=True)
        o_ref[rows, :] = jnp.where(mine, col, o_ref[rows, :])


def _attn_sample(page_table, q_t, bias, cache_k, cache_v, layer):
    n, n_pages = page_table.shape
    depth, n_pool = cache_k.shape[:2]
    ck = jnp.transpose(cache_k, (0, 1, 3, 4, 2)).reshape(depth, n_pool, C_WIDTH, PAGE_SIZE)
    cv = jnp.transpose(cache_v, (0, 1, 3, 4, 2)).reshape(depth, n_pool, C_WIDTH, PAGE_SIZE)

    def page(j):
        return pl.BlockSpec((1, 1, C_WIDTH, PAGE_SIZE), lambda b, pt: (layer, pt[b, j], 0, 0))

    whole = lambda shape: pl.BlockSpec(shape, lambda b, pt: (0,) * len(shape))
    pages = [page(j) for j in range(n_pages)]
    return pl.pallas_call(
        functools.partial(_attn_seq_kernel, n_pages=n_pages),
        grid_spec=pltpu.PrefetchScalarGridSpec(
            num_scalar_prefetch=1,
            grid=(n,),
            in_specs=[whole((C_WIDTH, n)), whole((C_HEADS, 1))] + pages + pages,
            out_specs=whole((C_WIDTH, n)),
            scratch_shapes=[pltpu.VMEM((C_WIDTH, PAGE_SIZE), F32),
                            pltpu.VMEM((C_HEADS, n_pages * PAGE_SIZE), F32),
                            pltpu.VMEM((C_HEADS, n_pages * PAGE_SIZE), F32)]),
        out_shape=jax.ShapeDtypeStruct((C_WIDTH, n), F32),
        compiler_params=_params(("arbitrary",)),
        name="attn_sample",
    )(page_table, q_t, bias.reshape(C_HEADS, 1), *([ck] * n_pages), *([cv] * n_pages))


def _ffn_kernel(x_ref, oa_ref, ob_ref, oc_ref, gate_ref, g1_ref, sh2_ref, sc2_ref, g2_ref,
                wa_ref, wb_ref, wc_ref, wo_ref, n2_ref, w1_ref, w2_ref, fn_ref, y_ref, *, final, ff_chunk,
                oc_channel_major):
    D = D_MODEL
    dot = functools.partial(jnp.dot, preferred_element_type=F32)
    oc = oc_ref[0].T if oc_channel_major else oc_ref[0]
    merged = (gate_ref[0, :, 0:D].astype(F32) * dot(oa_ref[0].astype(BF16), wa_ref[...])
              + gate_ref[0, :, D:2 * D].astype(F32) * dot(ob_ref[0].astype(BF16), wb_ref[...])
              + gate_ref[0, :, 2 * D:3 * D].astype(F32) * dot(oc.astype(BF16), wc_ref[...]))
    x1 = x_ref[0] + g1_ref[0] * dot(merged.astype(BF16), wo_ref[...])
    h2 = (_rms(x1, n2_ref[...]) * (1.0 + sc2_ref[0]) + sh2_ref[0]).astype(BF16)
    y = jnp.zeros_like(x1)
    for c in range(D_FF // ff_chunk):
        cs = slice(c * ff_chunk, (c + 1) * ff_chunk)
        a = jnp.maximum(dot(h2, w1_ref[:, cs]), 0.0)
        y = y + dot((a * a).astype(BF16), w2_ref[cs, :])
    x2 = x1 + g2_ref[0] * y
    y_ref[0] = _rms(x2, fn_ref[...]) if final else x2


def _ffn(x, oa, ob, oc, gate, mod, wa, wb, wc, wo, n2, w1, w2, fn, *, sample, tm, final):
    G, R, D = x.shape
    tok = lambda w: pl.BlockSpec((1, tm, w), lambda g, i: (g, i, 0))
    return pl.pallas_call(
        functools.partial(_ffn_kernel, final=final, ff_chunk=1024, oc_channel_major=sample),
        grid=(G, R // tm),
        in_specs=[tok(D), tok(512), tok(512),
                  pl.BlockSpec((1, 512, tm), lambda g, i: (g, 0, i)) if sample else tok(512), tok(3 * D),
                  _mod_spec(sample, tm, 2), _mod_spec(sample, tm, 3), _mod_spec(sample, tm, 4),
                  _mod_spec(sample, tm, 5),
                  _const_spec((512, D)), _const_spec((512, D)), _const_spec((512, D)), _const_spec((D, D)),
                  _const_spec((1, D)), _const_spec((D, D_FF)), _const_spec((D_FF, D)), _const_spec((1, D))],
        out_specs=tok(D),
        out_shape=jax.ShapeDtypeStruct((G, R, D), F32),
        compiler_params=_params(("parallel", "parallel")),
        name="ffn_sample" if sample else "ffn_prompt",
    )(x, oa, ob, oc, gate, mod, mod, mod, mod, wa, wb, wc, wo, n2, w1, w2, fn)


def _token_major(x):
    lead, R = x.shape[:-2], x.shape[-1]
    n = len(lead)
    return jnp.transpose(x.reshape(*lead, C_HEADS, C_HEAD_DIM, R), (*range(n), n + 2, n, n + 1))


def kernel(x_prompt, x_sample, c_prompt, c_sample, cache_k, cache_v, state_hgrn, page_table, w_ada, b_ada,
           norm1_w, norm2_w, w_in, gmlp_vnorm_w, gmlp_w_s, gmlp_b_s, sb_bias, hgrn_lb_logits, hgrn_onorm_w,
           w_branch_a, w_branch_b, w_branch_c, w_out, w_ff1, w_ff2, final_norm_w):
    depth = w_in.shape[0]
    nb, seq, d = x_prompt.shape
    ns = x_sample.shape[0]

    lb_cum = jnp.cumsum(jax.nn.softmax(hgrn_lb_logits.astype(F32), axis=0), axis=0)
    lower_bounds = lb_cum - lb_cum[0:1]

    n_c = nb + ns
    pad = (-n_c) % 8
    c_all = jnp.concatenate([c_prompt, c_sample, jnp.zeros((pad, d), F32)], axis=0)
    ada = _ada(c_all, w_ada, b_ada)

    bf = lambda w: w.astype(BF16)
    w_in16, wa16, wb16, wc16 = bf(w_in), bf(w_branch_a), bf(w_branch_b), bf(w_branch_c)
    wo16, w116, w216 = bf(w_out), bf(w_ff1), bf(w_ff2)
    fn = final_norm_w.reshape(1, d)
    gd = A_WIDTH // A_GROUPS

    yp = x_prompt
    ys = x_sample.reshape(1, ns, d)
    sp_l, ks_l, vs_l, gv_l = [], [], [], []
    for l in range(depth):
        final = l == depth - 1
        mod_p = ada[l, :nb].reshape(nb, 1, 6 * d)
        mod_s = ada[l, nb:n_c].reshape(1, ns, 6 * d)
        n1 = norm1_w[l].reshape(1, d)
        n2 = norm2_w[l].reshape(1, d)
        vn = gmlp_vnorm_w[l].reshape(1, A_WIDTH)
        lb = lower_bounds[l].reshape(1, 512)
        onorm = hgrn_onorm_w[l].reshape(1, B_WIDTH)
        b_tile = jnp.broadcast_to(gmlp_b_s[l][:, :, None], (A_GROUPS, A_CHUNK, gd))
        first = jnp.stack([jnp.repeat(gmlp_w_s[l, :, 0, 0], gd), jnp.repeat(gmlp_b_s[l, :, 0], gd)])

        oa, qb, lf, vb, gb, qt, kh, vt, k_p, v_p, gate = _inproj(
            yp, mod_p, n1, w_in16[l], vn, gmlp_w_s[l], b_tile, lb, sample=False, tm=256,
            layer=l, depth=depth, carried=() if l == 0 else (k_p, v_p))
        ob, s_fin = _hgrn_prompt(qb, lf, vb, gb, onorm, T=256)
        oc = _attn_prompt(sb_bias[l], qt, kh, vt, tq=ATT_TQ, nh=ATT_HEADS)
        yp = _ffn(yp, oa, ob, oc, gate, mod_p, wa16[l], wb16[l], wc16[l], wo16[l], n2, w116[l], w216[l], fn,
                  sample=False, tm=256, final=final)
        sp_l.append(s_fin)

        oa, qb, lf, vb, gb, qc, k_c, v_c, gate, va = _inproj(
            ys, mod_s, n1, w_in16[l], vn, first, b_tile[:1, :8], lb, sample=True, tm=ns)
        ob, s_all = _hgrn_sample(qb[0], lf[0], vb[0], gb[0], onorm, state_hgrn, l, n_seq=8,
                                 carried=() if l == 0 else (s_all,))
        oc = _attn_sample(page_table, qc[0], sb_bias[l], cache_k, cache_v, l)
        ys = _ffn(ys, oa, ob[None], oc[None], gate, mod_s, wa16[l], wb16[l], wc16[l], wo16[l],
                  n2, w116[l], w216[l], fn, sample=True, tm=ns, final=final)
        ks_l.append(_token_major(k_c[0]).reshape(ns, 1, C_HEADS, C_HEAD_DIM))
        vs_l.append(_token_major(v_c[0]).reshape(ns, 1, C_HEADS, C_HEAD_DIM))
        gv_l.append(va.reshape(ns, 1, A_WIDTH))

    return (yp, ys.reshape(ns, 1, d), _token_major(k_p), _token_major(v_p), jnp.stack(sp_l),
            jnp.stack(ks_l), jnp.stack(vs_l), s_all, jnp.stack(gv_l))
```

```python
import functools

import jax
import jax.numpy as jnp
from jax import lax
from jax.experimental import pallas as pl
from jax.experimental.pallas import tpu as pltpu

F32 = jnp.float32
BF16 = jnp.bfloat16

D_MODEL = 1024
A_WIDTH = 512
A_GROUPS = 4
A_CHUNK = 128
B_HEADS = 4
B_KDIM = 128
B_VDIM = 128
B_WIDTH = 512
C_HEADS = 8
C_HEAD_DIM = 64
C_WIDTH = 512
D_FF = 4096
PAGE_SIZE = 128
EPS = 1e-6

_SEG = dict(a_u=(0, 512), a_v=(512, 1024), b_q=(1024, 1536), b_f=(1536, 2048), b_i=(2048, 2560),
            b_g=(2560, 3072), c_q=(3072, 3584), c_k=(3584, 4096), c_v=(4096, 4608), gate=(4608, 7680))
IN_COLS = 7680

VMEM_LIMIT = 56 * 1024 * 1024
HGRN_SUB = 16
ATT_TQ = 256
ATT_HEADS = 8
LOG2E = 1.4426950408889634
MASKED_EXPONENT = -1e30


def _rms(x, w):
    return x * lax.rsqrt(jnp.mean(x * x, axis=-1, keepdims=True) + EPS) * w


def _sigmoid(x):
    return 0.5 * jnp.tanh(0.5 * x) + 0.5


def _silu(x):
    return x * _sigmoid(x)


def _softplus(z):
    return jnp.maximum(z, 0.0) + jnp.log(1.0 + jnp.exp(-jnp.abs(z)))


def _const_spec(shape):
    nd = len(shape)
    return pl.BlockSpec(shape, lambda *_: (0,) * nd, pipeline_mode=pl.Buffered(1))


def _params(sem):
    return pltpu.CompilerParams(dimension_semantics=sem, vmem_limit_bytes=VMEM_LIMIT)


def _ada_kernel(c_ref, w_ref, b_ref, o_ref):
    s = _silu(c_ref[...]).astype(BF16)
    o_ref[0] = jnp.dot(s, w_ref[0].astype(BF16), preferred_element_type=F32) + b_ref[0]


def _ada(c_all, w_ada, b_ada):
    depth, d, n = w_ada.shape
    m = c_all.shape[0]
    tn = 1536
    return pl.pallas_call(
        _ada_kernel,
        grid=(depth, n // tn),
        in_specs=[pl.BlockSpec((m, d), lambda l, j: (0, 0)),
                  pl.BlockSpec((1, d, tn), lambda l, j: (l, 0, j)),
                  pl.BlockSpec((1, 1, tn), lambda l, j: (l, 0, j))],
        out_specs=pl.BlockSpec((1, m, tn), lambda l, j: (l, 0, j)),
        out_shape=jax.ShapeDtypeStruct((depth, m, n), F32),
        compiler_params=_params(("arbitrary", "arbitrary")),
        name="ada",
    )(c_all, w_ada, b_ada.reshape(depth, 1, n))


def _inproj_kernel(x_ref, sh_ref, sc_ref, n1_ref, w_ref, vn_ref, ws_ref, bst_ref, lb_ref, *out_refs, tm, sample,
                   n_carried):
    out_refs = out_refs[n_carried:]
    if sample:
        oa_ref, qb_ref, lf_ref, vb_ref, gb_ref, qc_ref, k_ref, v_ref, gate_ref, va_ref = out_refs
    else:
        oa_ref, qb_ref, lf_ref, vb_ref, gb_ref, qc_ref, kh_ref, vt_ref, k_ref, v_ref, gate_ref = out_refs
    act = qb_ref.dtype
    x = x_ref[0]
    h = (_rms(x, n1_ref[...]) * (1.0 + sc_ref[0]) + sh_ref[0]).astype(BF16)

    def seg(name):
        a, b = _SEG[name]
        return jnp.dot(h, w_ref[:, a:b], preferred_element_type=F32)

    u = jax.nn.gelu(seg("a_u"))
    va = _rms(jax.nn.gelu(seg("a_v")), vn_ref[...])
    gd = A_WIDTH // A_GROUPS
    if sample:
        va_ref[0] = va
        oa_ref[0] = u * (va * ws_ref[0:1, :] + ws_ref[1:2, :])
    else:
        row = lax.broadcasted_iota(jnp.int32, (A_CHUNK, A_CHUNK), 0)
        col = lax.broadcasted_iota(jnp.int32, (A_CHUNK, A_CHUNK), 1)
        va16 = va.astype(BF16)
        for g in range(A_GROUPS):
            cs = slice(g * gd, (g + 1) * gd)
            wg = jnp.where(col <= row, ws_ref[g], 0.0).astype(BF16)
            for c in range(tm // A_CHUNK):
                rs = slice(c * A_CHUNK, (c + 1) * A_CHUNK)
                mixed = jnp.dot(wg, va16[rs, cs], preferred_element_type=F32) + bst_ref[g]
                oa_ref[0, rs, cs] = (u[rs, cs] * mixed).astype(BF16)

    qb_ref[0] = _silu(seg("b_q")).astype(act)
    bf = seg("b_f")
    lb = lb_ref[...]
    log_sig = jnp.minimum(bf, 0.0) - jnp.log(1.0 + jnp.exp(-jnp.abs(bf)))
    a = jnp.log(lb)
    b = jnp.log(1.0 - lb) + log_sig
    lf_ref[0] = jnp.maximum(a, b) + jnp.log(1.0 + jnp.exp(-jnp.abs(a - b)))
    vb_ref[0] = seg("b_i").astype(act)
    gb_ref[0] = _silu(seg("b_g")).astype(act)

    cq = seg("c_q") * (C_HEAD_DIM ** -0.5 * (1.0 if sample else LOG2E))
    ck = seg("c_k")
    cv = seg("c_v")
    k_ref[...] = ck.T.reshape(k_ref.shape)
    cv_t = cv.T
    v_ref[...] = cv_t.reshape(v_ref.shape)
    qc_ref[0] = cq.T.astype(qc_ref.dtype)
    if not sample:
        vt_ref[0] = cv_t.astype(BF16)
        for hh in range(C_HEADS):
            kh_ref[0, hh] = ck[:, hh * C_HEAD_DIM:(hh + 1) * C_HEAD_DIM].astype(BF16)

    gate_ref[0] = _sigmoid(seg("gate")).astype(BF16)


def _mod_spec(sample, tm, k):
    if sample:
        return pl.BlockSpec((1, tm, D_MODEL), lambda g, i: (g, i, k))
    return pl.BlockSpec((1, 1, D_MODEL), lambda g, i: (g, 0, k))


def _inproj(x, mod, n1, w_in, vn, w_s, b_s, lb, *, sample, tm, layer=0, depth=1, carried=()):
    G, R, D = x.shape
    tok = lambda w: pl.BlockSpec((1, tm, w), lambda g, i: (g, i, 0))
    hm = pl.BlockSpec((1, C_HEADS, tm, C_HEAD_DIM), lambda g, i: (g, 0, i, 0))
    sd = jax.ShapeDtypeStruct
    act = F32 if sample else BF16
    w512 = lambda dt: sd((G, R, 512), dt)
    out_shape = [w512(act), w512(act), w512(F32), w512(act), w512(act)]
    out_specs = [tok(512)] * 5
    chan = pl.BlockSpec((1, 512, tm), lambda g, i: (g, 0, i))
    if sample:
        out_shape += [sd((G, 512, R), F32)]
        out_specs += [chan]
    else:
        out_shape += [sd((G, 512, R), BF16), sd((G, C_HEADS, R, C_HEAD_DIM), BF16), sd((G, 512, R), BF16)]
        out_specs += [chan, hm, chan]
    kv_index = len(out_shape)
    slab = pl.BlockSpec((1, 1, 512, tm), lambda g, i: (layer, g, 0, i))
    out_shape += [sd((depth, G, 512, R), F32), sd((depth, G, 512, R), F32), sd((G, R, 3 * D_MODEL), BF16)]
    out_specs += [slab, slab, tok(3 * D_MODEL)]
    if sample:
        out_shape += [w512(F32)]
        out_specs += [tok(512)]
    operands = (x, mod, mod, n1, w_in, vn, w_s, b_s, lb)
    return pl.pallas_call(
        functools.partial(_inproj_kernel, tm=tm, sample=sample, n_carried=len(carried)),
        grid=(G, R // tm),
        in_specs=[tok(D), _mod_spec(sample, tm, 0), _mod_spec(sample, tm, 1),
                  _const_spec((1, D)), _const_spec((D, IN_COLS)), _const_spec((1, A_WIDTH)),
                  _const_spec(w_s.shape), _const_spec(b_s.shape), _const_spec((1, 512))]
        + [pl.BlockSpec(memory_space=pl.ANY)] * len(carried),
        out_specs=out_specs,
        out_shape=out_shape,
        input_output_aliases={len(operands) + n: kv_index + n for n in range(len(carried))},
        compiler_params=_params(("parallel", "parallel")),
        name="inproj_sample" if sample else "inproj_prompt",
    )(*operands, *carried)


def _hgrn_kernel(q_ref, lf_ref, v_ref, gb_ref, on_ref, ob_ref, s_ref, st_scr, a_scr, *, T, C):
    i = pl.program_id(1)

    @pl.when(i == 0)
    def _():
        st_scr[...] = jnp.zeros_like(st_scr)

    dot = functools.partial(jnp.dot, preferred_element_type=F32)
    nt = (((1,), (1,)), ((), ()))
    log2c = C.bit_length() - 1
    lanes = B_KDIM
    r_i = lax.broadcasted_iota(jnp.int32, (T, T), 0)
    c_i = lax.broadcasted_iota(jnp.int32, (T, T), 1)
    same_sub = lax.shift_right_logical(r_i, log2c) == lax.shift_right_logical(c_i, log2c)
    tri_block = jnp.where(c_i <= r_i, 1.0, 0.0).astype(BF16)
    tri_sub = jnp.where((c_i <= r_i) & same_sub, 1.0, 0.0).astype(BF16)

    g = lf_ref[0]
    g1 = g.astype(BF16)
    rem = g - g1.astype(F32)
    g2 = rem.astype(BF16)
    g3 = (rem - g2.astype(F32)).astype(BF16)
    b = dot(tri_block, g1) + dot(tri_block, g2) + dot(tri_block, g3)
    bs = dot(tri_sub, g1) + dot(tri_sub, g2) + dot(tri_sub, g3)
    f = jnp.exp(g)
    kk = 1.0 - f
    q = q_ref[0].astype(F32)
    v16 = v_ref[0]
    b_end = b[T - 1:T, :]
    q_block = (q * jnp.exp(b)).astype(BF16)
    q_sub = (q * jnp.exp(bs)).astype(BF16)
    k_dec = (kk * jnp.exp(b_end - b)).astype(BF16)
    decay = jnp.exp(b_end)
    gb = gb_ref[0].astype(F32)

    rl = lax.broadcasted_iota(jnp.int32, (T, lanes), 0) & (lanes - 1)
    cl = lax.broadcasted_iota(jnp.int32, (T, lanes), 1)
    offset = jnp.where(lax.shift_right_logical(rl, log2c) == lax.shift_right_logical(cl, log2c), cl - rl, 1)
    ones = jnp.ones((lanes, lanes), BF16)

    for h in range(B_HEADS):
        cs = slice(h * B_KDIM, (h + 1) * B_KDIM)
        st = st_scr[h]
        o = lax.dot_general(q_block[:, cs], st.astype(BF16), nt, preferred_element_type=F32)

        a_scr[...] = jnp.zeros_like(a_scr)
        for s in range(1, T // C):
            rows = slice(s * C, (s + 1) * C)
            ref_b = b[s * C - 1:s * C, cs]
            k_hat = (kk[0:s * C, cs] * jnp.exp(ref_b - b[0:s * C, cs])).astype(BF16)
            a_scr[rows, 0:s * C] = lax.dot_general(q_sub[rows, cs], k_hat, nt, preferred_element_type=F32)

        qh, fh = q[:, cs], f[:, cs]
        k_decayed = kk[:, cs]
        diag = jnp.zeros((T, lanes), F32)
        for delta in range(C):
            if delta > 0:
                k_decayed = fh * pltpu.roll(k_decayed, 1, axis=0)
            score = dot((qh * k_decayed).astype(BF16), ones)
            diag = jnp.where(offset == -delta, score, diag)
        for d in range(T // lanes):
            sl = slice(d * lanes, (d + 1) * lanes)
            a_scr[sl, sl] += diag[sl, :]

        o = o + dot(a_scr[...].astype(BF16), v16[:, cs])
        ob_ref[0, :, cs] = (_rms(o, on_ref[:, cs]) * gb[:, cs]).astype(BF16)
        upd = lax.dot_general(v16[:, cs], k_dec[:, cs], (((0,), (0,)), ((), ())), preferred_element_type=F32)
        st_scr[h] = st * decay[:, cs] + upd

    @pl.when(i == pl.num_programs(1) - 1)
    def _():
        for h in range(B_HEADS):
            s_ref[0, h] = st_scr[h].T


def _hgrn_prompt(q, lf, v, gb, onorm, *, T):
    G, R, W = q.shape
    tok = pl.BlockSpec((1, T, W), lambda g, i: (g, i, 0))
    return pl.pallas_call(
        functools.partial(_hgrn_kernel, T=T, C=HGRN_SUB),
        grid=(G, R // T),
        in_specs=[tok, tok, tok, tok, _const_spec((1, W))],
        out_specs=[tok, pl.BlockSpec((1, B_HEADS, B_KDIM, B_VDIM), lambda g, i: (g, 0, 0, 0))],
        out_shape=[jax.ShapeDtypeStruct((G, R, W), BF16),
                   jax.ShapeDtypeStruct((G, B_HEADS, B_KDIM, B_VDIM), F32)],
        scratch_shapes=[pltpu.VMEM((B_HEADS, B_VDIM, B_KDIM), F32), pltpu.VMEM((T, T), F32)],
        compiler_params=_params(("parallel", "arbitrary")),
        name="hgrn_prompt",
    )(q, lf, v, gb, onorm)


def _hgrn_step_kernel(q_ref, lf_ref, v_ref, gb_ref, on_ref, s0_ref, *refs, n_seq):
    ob_ref, s1_ref = refs[-2:]
    eye = (lax.broadcasted_iota(jnp.int32, (B_KDIM, B_KDIM), 0)
           == lax.broadcasted_iota(jnp.int32, (B_KDIM, B_KDIM), 1))
    f_all = jnp.exp(lf_ref[...])
    for s in range(n_seq):
        row = slice(s, s + 1)
        for h in range(B_HEADS):
            cs = slice(h * B_KDIM, (h + 1) * B_KDIM)
            f_col = jnp.sum(jnp.where(eye, jnp.broadcast_to(f_all[row, cs], (B_KDIM, B_KDIM)), 0.0),
                            axis=1, keepdims=True)
            s_new = f_col * s0_ref[0, s, h] + (1.0 - f_col) * v_ref[row, cs]
            s1_ref[0, s, h] = s_new
            q8 = jnp.broadcast_to(q_ref[row, cs], (8, B_KDIM)).astype(BF16)
            o = jnp.dot(q8, s_new.astype(BF16), preferred_element_type=F32)[0:1, :]
            ob_ref[row, cs] = _rms(o, on_ref[:, cs]) * gb_ref[row, cs]


def _hgrn_sample(q, lf, v, gb, onorm, state, layer, *, n_seq, carried=()):
    n, W = q.shape
    rows = pl.BlockSpec((n_seq, W), lambda b: (b, 0))
    st = pl.BlockSpec((1, n_seq, B_HEADS, B_KDIM, B_VDIM), lambda b: (layer, b, 0, 0, 0))
    operands = (q, lf, v, gb, onorm, state)
    return pl.pallas_call(
        functools.partial(_hgrn_step_kernel, n_seq=n_seq),
        grid=(n // n_seq,),
        in_specs=[rows, rows, rows, rows, pl.BlockSpec((1, W), lambda b: (0, 0)), st]
        + [pl.BlockSpec(memory_space=pl.ANY)] * len(carried),
        out_specs=[rows, st],
        out_shape=[jax.ShapeDtypeStruct((n, W), F32), jax.ShapeDtypeStruct(state.shape, F32)],
        input_output_aliases={len(operands) + n_: 1 + n_ for n_ in range(len(carried))},
        compiler_params=_params(("parallel",)),
        name="hgrn_sample",
    )(*operands, *carried)


def _attn_kernel(bias_ref, qt_ref, k_ref, vt_ref, o_ref, z_scr, x_scr, acc_scr, *, tq, nh):
    hp = pl.program_id(1)
    i = pl.program_id(2)
    dh = C_HEAD_DIM
    pad = 16
    r_i = lax.broadcasted_iota(jnp.int32, (tq + pad, tq), 0)
    c_i = lax.broadcasted_iota(jnp.int32, (tq + pad, tq), 1)
    suffix = jnp.where((c_i > r_i) | (r_i >= tq), 1.0, 0.0).astype(BF16)
    keep = (lax.broadcasted_iota(jnp.int32, (tq, tq), 0)
            < lax.broadcasted_iota(jnp.int32, (tq, tq), 1))
    heads = range(nh)

    def logits(j):
        r0 = pl.multiple_of(j * tq, tq)
        for hh in heads:
            kb = k_ref[0, hh, pl.ds(r0, tq), :]
            z = jnp.dot(kb, qt_ref[0, hh * dh:(hh + 1) * dh, :], preferred_element_type=F32)
            z_scr[hh] = z + bias_ref[nh * hp + hh] * LOG2E

    def exponents(later, diag):
        out = []
        for hh in heads:
            z = z_scr[hh]
            sp = jnp.maximum(z, 0.0) + jnp.log2(1.0 + jnp.exp2(-jnp.abs(z)))
            if diag:
                sp = jnp.where(keep, sp, 0.0)
            sums = jnp.dot(suffix, sp.astype(BF16), preferred_element_type=F32)
            x = z - sp - sums[0:tq] - later[hh]
            x_scr[hh] = jnp.where(keep, x, MASKED_EXPONENT) if diag else x
            out.append(later[hh] + sums[tq:tq + 1])
        return out

    def weigh(j):
        r0 = pl.multiple_of(j * tq, tq)
        for hh in heads:
            vb = vt_ref[0, hh * dh:(hh + 1) * dh, pl.ds(r0, tq)]
            acc_scr[hh] += jnp.dot(vb, jnp.exp2(x_scr[hh]).astype(BF16), preferred_element_type=F32)

    acc_scr[...] = jnp.zeros_like(acc_scr)
    logits(i)
    later = exponents([jnp.zeros((1, tq), F32) for hh in heads], True)
    logits(jnp.maximum(i - 1, 0))

    def body(m, later):
        weigh(i - m + 2)
        later = exponents(later, False)
        logits(i - m)
        return later

    later = lax.fori_loop(2, i + 1, body, later)

    @pl.when(i >= 1)
    def _():
        weigh(1)
        exponents(later, False)

    weigh(0)
    o_ref[0] = jnp.concatenate([acc_scr[hh].T for hh in heads], axis=1).astype(BF16)


def _attn_prompt(bias, q_t, k, v_t, *, tq, nh):
    G, H, R, Dh = k.shape
    return pl.pallas_call(
        functools.partial(_attn_kernel, tq=tq, nh=nh),
        grid=(G, H // nh, R // tq),
        in_specs=[pl.BlockSpec(memory_space=pltpu.SMEM),
                  pl.BlockSpec((1, nh * Dh, tq), lambda g, hp, i: (g, hp, i)),
                  pl.BlockSpec((1, nh, R, Dh), lambda g, hp, i: (g, hp, 0, 0)),
                  pl.BlockSpec((1, nh * Dh, R), lambda g, hp, i: (g, hp, 0))],
        out_specs=pl.BlockSpec((1, tq, nh * Dh), lambda g, hp, i: (g, i, hp)),
        out_shape=jax.ShapeDtypeStruct((G, R, H * Dh), BF16),
        scratch_shapes=[pltpu.VMEM((nh, tq, tq), F32), pltpu.VMEM((nh, tq, tq), F32),
                        pltpu.VMEM((nh, Dh, tq), F32)],
        compiler_params=_params(("parallel", "parallel", "arbitrary")),
        name="attn_prompt",
    )(bias, q_t, k, v_t)


def _attn_seq_kernel(pt_ref, qt_ref, bias_ref, *refs, n_pages):
    k_refs = refs[:n_pages]
    v_refs = refs[n_pages:2 * n_pages]
    o_ref, qb_scr, z_scr, w_scr = refs[2 * n_pages:]
    b = pl.program_id(0)
    ns = qt_ref.shape[1]
    sub = 8
    groups = C_HEAD_DIM // sub

    @pl.when(b == 0)
    def _():
        o_ref[...] = jnp.zeros_like(o_ref)

    mine = lax.broadcasted_iota(jnp.int32, (sub, ns), 1) == b
    for r in range(C_WIDTH // sub):
        rows = slice(r * sub, (r + 1) * sub)
        col = jnp.sum(jnp.where(mine, qt_ref[rows, :], 0.0), axis=1, keepdims=True)
        qb_scr[rows, :] = jnp.broadcast_to(col, (sub, PAGE_SIZE))

    for p in range(n_pages):
        for h in range(C_HEADS):
            part = None
            for r in range(groups):
                rows = slice(h * C_HEAD_DIM + r * sub, h * C_HEAD_DIM + (r + 1) * sub)
                t = k_refs[p][0, 0, rows, :] * qb_scr[rows, :]
                part = t if part is None else part + t
            z_scr[h:h + 1, p * PAGE_SIZE:(p + 1) * PAGE_SIZE] = jnp.sum(part, axis=0, keepdims=True)

    z = z_scr[...] + bias_ref[...]
    sp = _softplus(z)
    log_beta = z - sp
    r_i = lax.broadcasted_iota(jnp.int32, (PAGE_SIZE, PAGE_SIZE), 0)
    c_i = lax.broadcasted_iota(jnp.int32, (PAGE_SIZE, PAGE_SIZE), 1)
    upper = jnp.where(r_i > c_i, 1.0, 0.0).astype(BF16)
    later = jnp.zeros((C_HEADS, 1), F32)
    for p in reversed(range(n_pages)):
        ps = slice(p * PAGE_SIZE, (p + 1) * PAGE_SIZE)
        sp_hi = sp[:, ps].astype(BF16)
        sp_lo = (sp[:, ps] - sp_hi.astype(F32)).astype(BF16)
        within = (jnp.dot(sp_hi, upper, preferred_element_type=F32)
                  + jnp.dot(sp_lo, upper, preferred_element_type=F32))
        w_scr[:, ps] = jnp.exp(log_beta[:, ps] - within - later)
        later = later + jnp.sum(sp[:, ps], axis=1, keepdims=True)

    for r in range(C_WIDTH // sub):
        rows = slice(r * sub, (r + 1) * sub)
        h = r // groups
        acc = None
        for p in range(n_pages):
            wb = jnp.broadcast_to(w_scr[h:h + 1, p * PAGE_SIZE:(p + 1) * PAGE_SIZE], (sub, PAGE_SIZE))
            t = v_refs[p][0, 0, rows, :] * wb
            acc = t if acc is None else acc + t
        col = jnp.sum(acc, axis=1, keepdims=True)
        o_ref[rows, :] = jnp.where(mine, col, o_ref[rows, :])


def _attn_sample(page_table, q_t, bias, cache_k, cache_v, layer):
    n, n_pages = page_table.shape
    depth, n_pool = cache_k.shape[:2]
    ck = jnp.transpose(cache_k, (0, 1, 3, 4, 2)).reshape(depth, n_pool, C_WIDTH, PAGE_SIZE)
    cv = jnp.transpose(cache_v, (0, 1, 3, 4, 2)).reshape(depth, n_pool, C_WIDTH, PAGE_SIZE)

    def page(j):
        return pl.BlockSpec((1, 1, C_WIDTH, PAGE_SIZE), lambda b, pt: (layer, pt[b, j], 0, 0))

    whole = lambda shape: pl.BlockSpec(shape, lambda b, pt: (0,) * len(shape))
    pages = [page(j) for j in range(n_pages)]
    return pl.pallas_call(
        functools.partial(_attn_seq_kernel, n_pages=n_pages),
        grid_spec=pltpu.PrefetchScalarGridSpec(
            num_scalar_prefetch=1,
            grid=(n,),
            in_specs=[whole((C_WIDTH, n)), whole((C_HEADS, 1))] + pages + pages,
            out_specs=whole((C_WIDTH, n)),
            scratch_shapes=[pltpu.VMEM((C_WIDTH, PAGE_SIZE), F32),
                            pltpu.VMEM((C_HEADS, n_pages * PAGE_SIZE), F32),
                            pltpu.VMEM((C_HEADS, n_pages * PAGE_SIZE), F32)]),
        out_shape=jax.ShapeDtypeStruct((C_WIDTH, n), F32),
        compiler_params=_params(("arbitrary",)),
        name="attn_sample",
    )(page_table, q_t, bias.reshape(C_HEADS, 1), *([ck] * n_pages), *([cv] * n_pages))


def _ffn_kernel(x_ref, oa_ref, ob_ref, oc_ref, gate_ref, g1_ref, sh2_ref, sc2_ref, g2_ref,
                wa_ref, wb_ref, wc_ref, wo_ref, n2_ref, w1_ref, w2_ref, fn_ref, y_ref, *, final, ff_chunk,
                oc_channel_major):
    D = D_MODEL
    dot = functools.partial(jnp.dot, preferred_element_type=F32)
    oc = oc_ref[0].T if oc_channel_major else oc_ref[0]
    merged = (gate_ref[0, :, 0:D].astype(F32) * dot(oa_ref[0].astype(BF16), wa_ref[...])
              + gate_ref[0, :, D:2 * D].astype(F32) * dot(ob_ref[0].astype(BF16), wb_ref[...])
              + gate_ref[0, :, 2 * D:3 * D].astype(F32) * dot(oc.astype(BF16), wc_ref[...]))
    x1 = x_ref[0] + g1_ref[0] * dot(merged.astype(BF16), wo_ref[...])
    h2 = (_rms(x1, n2_ref[...]) * (1.0 + sc2_ref[0]) + sh2_ref[0]).astype(BF16)
    y = jnp.zeros_like(x1)
    for c in range(D_FF // ff_chunk):
        cs = slice(c * ff_chunk, (c + 1) * ff_chunk)
        a = jnp.maximum(dot(h2, w1_ref[:, cs]), 0.0)
        y = y + dot((a * a).astype(BF16), w2_ref[cs, :])
    x2 = x1 + g2_ref[0] * y
    y_ref[0] = _rms(x2, fn_ref[...]) if final else x2


def _ffn(x, oa, ob, oc, gate, mod, wa, wb, wc, wo, n2, w1, w2, fn, *, sample, tm, final):
    G, R, D = x.shape
    tok = lambda w: pl.BlockSpec((1, tm, w), lambda g, i: (g, i, 0))
    return pl.pallas_call(
        functools.partial(_ffn_kernel, final=final, ff_chunk=1024, oc_channel_major=sample),
        grid=(G, R // tm),
        in_specs=[tok(D), tok(512), tok(512),
                  pl.BlockSpec((1, 512, tm), lambda g, i: (g, 0, i)) if sample else tok(512), tok(3 * D),
                  _mod_spec(sample, tm, 2), _mod_spec(sample, tm, 3), _mod_spec(sample, tm, 4),
                  _mod_spec(sample, tm, 5),
                  _const_spec((512, D)), _const_spec((512, D)), _const_spec((512, D)), _const_spec((D, D)),
                  _const_spec((1, D)), _const_spec((D, D_FF)), _const_spec((D_FF, D)), _const_spec((1, D))],
        out_specs=tok(D),
        out_shape=jax.ShapeDtypeStruct((G, R, D), F32),
        compiler_params=_params(("parallel", "parallel")),
        name="ffn_sample" if sample else "ffn_prompt",
    )(x, oa, ob, oc, gate, mod, mod, mod, mod, wa, wb, wc, wo, n2, w1, w2, fn)


def _token_major(x):
    lead, R = x.shape[:-2], x.shape[-1]
    n = len(lead)
    return jnp.transpose(x.reshape(*lead, C_HEADS, C_HEAD_DIM, R), (*range(n), n + 2, n, n + 1))


def kernel(x_prompt, x_sample, c_prompt, c_sample, cache_k, cache_v, state_hgrn, page_table, w_ada, b_ada,
           norm1_w, norm2_w, w_in, gmlp_vnorm_w, gmlp_w_s, gmlp_b_s, sb_bias, hgrn_lb_logits, hgrn_onorm_w,
           w_branch_a, w_branch_b, w_branch_c, w_out, w_ff1, w_ff2, final_norm_w):
    depth = w_in.shape[0]
    nb, seq, d = x_prompt.shape
    ns = x_sample.shape[0]

    lb_cum = jnp.cumsum(jax.nn.softmax(hgrn_lb_logits.astype(F32), axis=0), axis=0)
    lower_bounds = lb_cum - lb_cum[0:1]

    n_c = nb + ns
    pad = (-n_c) % 8
    c_all = jnp.concatenate([c_prompt, c_sample, jnp.zeros((pad, d), F32)], axis=0)
    ada = _ada(c_all, w_ada, b_ada)

    bf = lambda w: w.astype(BF16)
    w_in16, wa16, wb16, wc16 = bf(w_in), bf(w_branch_a), bf(w_branch_b), bf(w_branch_c)
    wo16, w116, w216 = bf(w_out), bf(w_ff1), bf(w_ff2)
    fn = final_norm_w.reshape(1, d)
    gd = A_WIDTH // A_GROUPS

    yp = x_prompt
    ys = x_sample.reshape(1, ns, d)
    sp_l, ks_l, vs_l, gv_l = [], [], [], []
    for l in range(depth):
        final = l == depth - 1
        mod_p = ada[l, :nb].reshape(nb, 1, 6 * d)
        mod_s = ada[l, nb:n_c].reshape(1, ns, 6 * d)
        n1 = norm1_w[l].reshape(1, d)
        n2 = norm2_w[l].reshape(1, d)
        vn = gmlp_vnorm_w[l].reshape(1, A_WIDTH)
        lb = lower_bounds[l].reshape(1, 512)
        onorm = hgrn_onorm_w[l].reshape(1, B_WIDTH)
        b_tile = jnp.broadcast_to(gmlp_b_s[l][:, :, None], (A_GROUPS, A_CHUNK, gd))
        first = jnp.stack([jnp.repeat(gmlp_w_s[l, :, 0, 0], gd), jnp.repeat(gmlp_b_s[l, :, 0], gd)])

        oa, qb, lf, vb, gb, qt, kh, vt, k_p, v_p, gate = _inproj(
            yp, mod_p, n1, w_in16[l], vn, gmlp_w_s[l], b_tile, lb, sample=False, tm=512,
            layer=l, depth=depth, carried=() if l == 0 else (k_p, v_p))
        ob, s_fin = _hgrn_prompt(qb, lf, vb, gb, onorm, T=256)
        oc = _attn_prompt(sb_bias[l], qt, kh, vt, tq=ATT_TQ, nh=ATT_HEADS)
        yp = _ffn(yp, oa, ob, oc, gate, mod_p, wa16[l], wb16[l], wc16[l], wo16[l], n2, w116[l], w216[l], fn,
                  sample=False, tm=512, final=final)
        sp_l.append(s_fin)

        oa, qb, lf, vb, gb, qc, k_c, v_c, gate, va = _inproj(
            ys, mod_s, n1, w_in16[l], vn, first, b_tile[:1, :8], lb, sample=True, tm=ns)
        ob, s_all = _hgrn_sample(qb[0], lf[0], vb[0], gb[0], onorm, state_hgrn, l, n_seq=8,
                                 carried=() if l == 0 else (s_all,))
        oc = _attn_sample(page_table, qc[0], sb_bias[l], cache_k, cache_v, l)
        ys = _ffn(ys, oa, ob[None], oc[None], gate, mod_s, wa16[l], wb16[l], wc16[l], wo16[l],
                  n2, w116[l], w216[l], fn, sample=True, tm=ns, final=final)
        ks_l.append(_token_major(k_c[0]).reshape(ns, 1, C_HEADS, C_HEAD_DIM))
        vs_l.append(_token_major(v_c[0]).reshape(ns, 1, C_HEADS, C_HEAD_DIM))
        gv_l.append(va.reshape(ns, 1, A_WIDTH))

    return (yp, ys.reshape(ns, 1, d), _token_major(k_p), _token_major(v_p), jnp.stack(sp_l),
            jnp.stack(ks_l), jnp.stack(vs_l), s_all, jnp.stack(gv_l))
```
